```python
import jax, jax.numpy as jnp
from jax import lax
import numpy as np

D_MODEL = 1024
BATCH = 4
SEQ = 4096
DEPTH = 4

EPS = 1e-6
N_BRANCH = 4
BRANCH_WIDTH = D_MODEL // 2
GM_HEADS = 4
GM_CHUNK = 128
POOL_WINDOWS = (2, 4, 8, 16)
POOL_GROUP = BRANCH_WIDTH // len(POOL_WINDOWS)
ATT_HEADS = 4
ATT_HEAD_DIM = BRANCH_WIDTH // ATT_HEADS
DIL_PATTERNS = ((128, 1), (512, 4), (2048, 16))
N_DIL = len(DIL_PATTERNS)
ATT_BLOCK = 128
MEM_LEN = 256
MEM_HEADS = 4
MEM_HEAD_DIM = BRANCH_WIDTH // MEM_HEADS
NEG = -1e30

IN_SIZES = (2 * BRANCH_WIDTH, BRANCH_WIDTH,
            BRANCH_WIDTH, BRANCH_WIDTH,
            N_DIL * BRANCH_WIDTH, BRANCH_WIDTH, BRANCH_WIDTH, BRANCH_WIDTH,
            BRANCH_WIDTH, BRANCH_WIDTH,
            N_BRANCH * D_MODEL)
D_IN = sum(IN_SIZES)

kernel_name = "hybrid_gmlp_pool_dilated_attn_mem"


def rms_norm(x, g):
    xf = x.astype(jnp.float32)
    y = xf * lax.rsqrt(jnp.mean(xf * xf, axis=-1, keepdims=True) + EPS)
    return (y * g.astype(jnp.float32)).astype(x.dtype)


def layer_norm(x, g, b):
    xf = x.astype(jnp.float32)
    mu = jnp.mean(xf, axis=-1, keepdims=True)
    var = jnp.mean(jnp.square(xf - mu), axis=-1, keepdims=True)
    y = (xf - mu) * lax.rsqrt(var + EPS)
    return (y * g.astype(jnp.float32) + b.astype(jnp.float32)).astype(x.dtype)


def split_cols(h):
    idx = np.cumsum(np.array(IN_SIZES))[:-1].tolist()
    return jnp.split(h, idx, axis=-1)


def gmlp_spatial_gating(uv, ln_g, ln_b, w_s, b_s):
    u, v = jnp.split(jax.nn.gelu(uv, approximate=False), 2, axis=-1)
    v = layer_norm(v, ln_g, ln_b)
    B_, S_, W = v.shape
    nc = S_ // GM_CHUNK
    vh = v.reshape(B_, nc, GM_CHUNK, GM_HEADS, W // GM_HEADS)
    causal = jnp.tril(jnp.ones((GM_CHUNK, GM_CHUNK), dtype=bool))
    w = jnp.where(causal[None], w_s, jnp.zeros_like(w_s)).astype(v.dtype)
    mixed = jnp.einsum('hts,bcshe->bcthe', w, vh) + b_s.T.astype(v.dtype)[None, None, :, :, None]
    return u * mixed.reshape(B_, S_, W)


def multiscale_pool(p, pool_w, pool_scale):
    B_, S_, W = p.shape
    pf = p.astype(jnp.float32)
    cs = jnp.cumsum(pf, axis=1)
    count = jnp.arange(1, S_ + 1, dtype=jnp.float32)
    outs = []
    for g, win in enumerate(POOL_WINDOWS):
        c = cs[..., g * POOL_GROUP:(g + 1) * POOL_GROUP]
        prev = jnp.pad(c, ((0, 0), (win, 0), (0, 0)))[:, :S_]
        mean = (c - prev) / jnp.minimum(count, float(win))[None, :, None]
        outs.append(mean - pf[..., g * POOL_GROUP:(g + 1) * POOL_GROUP])
    d = jnp.stack(outs, axis=2)
    y = jnp.einsum('bsgi,gio->bsgo', d, pool_w.astype(jnp.float32)).reshape(B_, S_, W)
    return (y * pool_scale.astype(jnp.float32)).astype(p.dtype)


def dilated_window_attention(q, k, v, window, dilation):
    B_, S_, H, E = q.shape
    n_back = window // dilation
    L = S_ // dilation
    nb = -(-L // ATT_BLOCK)
    Lp = nb * ATT_BLOCK

    def to_blocks(t):
        t = t.reshape(B_, L, dilation, H, E).transpose(0, 2, 1, 3, 4)
        t = jnp.pad(t, ((0, 0), (0, 0), (0, Lp - L), (0, 0), (0, 0)))
        return t.reshape(B_, dilation, nb, ATT_BLOCK, H, E)

    def with_prev(t):
        prev = jnp.pad(t, ((0, 0), (0, 0), (1, 0), (0, 0), (0, 0), (0, 0)))[:, :, :nb]
        return jnp.concatenate([prev, t], axis=3)

    qb = to_blocks(q).astype(jnp.float32)
    kk = with_prev(to_blocks(k)).astype(jnp.float32)
    vv = with_prev(to_blocks(v)).astype(jnp.float32)
    s = jnp.einsum('brnqhe,brnkhe->brnhqk', qb, kk) * (E ** -0.5)
    i = jnp.arange(ATT_BLOCK)[:, None]
    j = jnp.arange(2 * ATT_BLOCK)[None, :]
    dist = ATT_BLOCK + i - j
    band = (dist >= 0) & (dist <= n_back)
    key_exists = (jnp.arange(nb)[:, None] > 0) | (jnp.arange(2 * ATT_BLOCK)[None, :] >= ATT_BLOCK)
    valid = band[None] & key_exists[:, None, :]
    s = jnp.where(valid[None, None, :, None], s, NEG)
    m = jnp.max(s, axis=-1, keepdims=True)
    e = jnp.exp(s - m)
    den = jnp.sum(e, axis=-1, keepdims=True)
    o = jnp.einsum('brnhqk,brnkhe->brnqhe', e / den, vv)
    lse = (m + jnp.log(den))[..., 0]
    o = o.reshape(B_, dilation, Lp, H, E)[:, :, :L].transpose(0, 2, 1, 3, 4).reshape(B_, S_, H, E)
    lse = lse.transpose(0, 1, 2, 4, 3).reshape(B_, dilation, Lp, H)[:, :, :L]
    lse = lse.transpose(0, 2, 1, 3).reshape(B_, S_, H)
    return o, lse


def dilated_mixture(c_q, c_k, c_v):
    B_, S_, _ = c_q.shape
    qg = c_q.reshape(B_, S_, N_DIL, ATT_HEADS, ATT_HEAD_DIM)
    k = c_k.reshape(B_, S_, ATT_HEADS, ATT_HEAD_DIM)
    v = c_v.reshape(B_, S_, ATT_HEADS, ATT_HEAD_DIM)
    outs, lses = [], []
    for g, (win, dil) in enumerate(DIL_PATTERNS):
        o, l = dilated_window_attention(qg[:, :, g], k, v, win, dil)
        outs.append(o)
        lses.append(l)
    alpha = jax.nn.softmax(jnp.stack(lses, axis=0), axis=0)
    o = jnp.sum(alpha[..., None] * jnp.stack(outs, axis=0), axis=0)
    return o.reshape(B_, S_, BRANCH_WIDTH).astype(c_q.dtype)


def memory_attention(m_q, mem_n, w_kv):
    B_, S_, _ = m_q.shape
    q = m_q.reshape(B_, S_, MEM_HEADS, MEM_HEAD_DIM).astype(jnp.float32)
    k, v = jnp.split(mem_n @ w_kv, 2, axis=-1)
    k = k.reshape(B_, -1, MEM_HEADS, MEM_HEAD_DIM).astype(jnp.float32)
    v = v.reshape(B_, -1, MEM_HEADS, MEM_HEAD_DIM).astype(jnp.float32)
    s = jnp.einsum('bshe,bmhe->bhsm', q, k) * (MEM_HEAD_DIM ** -0.5)
    p = jax.nn.softmax(s, axis=-1)
    o = jnp.einsum('bhsm,bmhe->bshe', p, v)
    return o.reshape(B_, S_, BRANCH_WIDTH).astype(m_q.dtype)


def setup_inputs(seed: int = 0) -> dict:
    key = jax.random.key(seed)
    ks = jax.random.split(key, 20)
    f32 = jnp.float32
    nrm = lambda k, shape, s: jax.random.normal(k, shape, f32) * s
    return {
        "x": nrm(ks[0], (BATCH, SEQ, D_MODEL), 1.0),
        "mem": nrm(ks[1], (BATCH, MEM_LEN, D_MODEL), 1.0),
        "norm_g": 1.0 + nrm(ks[2], (DEPTH, D_MODEL), 0.02),
        "w_in": nrm(ks[3], (DEPTH, D_MODEL, D_IN), D_MODEL ** -0.5),
        "gm_ln_g": 1.0 + nrm(ks[4], (DEPTH, BRANCH_WIDTH), 0.02),
        "gm_ln_b": nrm(ks[5], (DEPTH, BRANCH_WIDTH), 0.02),
        "gm_ws": nrm(ks[6], (DEPTH, GM_HEADS, GM_CHUNK, GM_CHUNK), GM_CHUNK ** -0.5),
        "gm_bs": 1.0 + nrm(ks[7], (DEPTH, GM_HEADS, GM_CHUNK), 0.1),
        "pool_w": nrm(ks[8], (DEPTH, len(POOL_WINDOWS), POOL_GROUP, POOL_GROUP), POOL_GROUP ** -0.5),
        "pool_scale": 1.0 + nrm(ks[9], (DEPTH, BRANCH_WIDTH), 0.1),
        "mem_norm_g": 1.0 + nrm(ks[10], (DEPTH, D_MODEL), 0.02),
        "w_mem_kv": nrm(ks[11], (DEPTH, D_MODEL, 2 * BRANCH_WIDTH), D_MODEL ** -0.5),
        "w_branch": nrm(ks[12], (DEPTH, N_BRANCH, BRANCH_WIDTH, D_MODEL), BRANCH_WIDTH ** -0.5),
        "w_out": nrm(ks[13], (DEPTH, D_MODEL, D_MODEL), 0.5 * D_MODEL ** -0.5),
        "final_norm_g": 1.0 + nrm(ks[14], (D_MODEL,), 0.02),
    }


def reference(x, mem, norm_g, w_in, gm_ln_g, gm_ln_b, gm_ws, gm_bs, pool_w, pool_scale,
              mem_norm_g, w_mem_kv, w_branch, w_out, final_norm_g):
    B_, S_, D = x.shape
    for l in range(DEPTH):
        h = rms_norm(x, norm_g[l])
        proj = h @ w_in[l]
        (a_uv, a_gate, p_in, p_gate, c_q, c_k, c_v, c_gate,
         m_q, m_gate, g_merge) = split_cols(proj)
        y_a = gmlp_spatial_gating(a_uv, gm_ln_g[l], gm_ln_b[l], gm_ws[l], gm_bs[l]) * jax.nn.silu(a_gate)
        y_p = multiscale_pool(p_in, pool_w[l], pool_scale[l]) * jax.nn.silu(p_gate)
        y_c = dilated_mixture(c_q, c_k, c_v) * jax.nn.silu(c_gate)
        mem_n = rms_norm(mem, mem_norm_g[l])
        y_m = memory_attention(m_q, mem_n, w_mem_kv[l]) * jax.nn.silu(m_gate)
        gates = jax.nn.sigmoid(g_merge.reshape(B_, S_, N_BRANCH, D))
        z = (gates[:, :, 0] * (y_a @ w_branch[l, 0])
             + gates[:, :, 1] * (y_p @ w_branch[l, 1])
             + gates[:, :, 2] * (y_c @ w_branch[l, 2])
             + gates[:, :, 3] * (y_m @ w_branch[l, 3]))
        x = x + z @ w_out[l]
    return rms_norm(x, final_norm_g)
```

```python
import functools

import jax
import jax.numpy as jnp
from jax import lax
from jax.experimental import pallas as pl
from jax.experimental.pallas import tpu as pltpu

F32 = jnp.float32
BF16 = jnp.bfloat16

EPS = 1e-6
NEG = -1e30
D_MODEL = 1024
WIDTH = 512
N_BRANCH = 4
HEADS = 4
HEAD_DIM = WIDTH // HEADS
CHUNK = 128
POOL_WINDOWS = (2, 4, 8, 16)
POOL_HALO = 16
DIL_PATTERNS = ((128, 1), (512, 4), (2048, 16))
MEM_LEN = 256

CB_GATES = 0
CB_AU, CB_AV, CB_AG = 8, 9, 10
CB_PI, CB_PG = 11, 12
CB_CQ, CB_CK, CB_CV, CB_CG = 13, 16, 17, 18
CB_MQ, CB_MG = 19, 20
N_CB = 21
D_IN = N_CB * WIDTH
N_GATE_COLS = N_BRANCH * D_MODEL
ORIG_GATE_START = D_IN - N_GATE_COLS

VMEM_LIMIT = 56 * 1024 * 1024

PROJ_TM = 1024
PROJ_TN = 1536
POST_T = 256
ATT_TQ = {1: 512, 4: 512, 16: 256}


def _dot(a, b):
    return jnp.dot(a, b, preferred_element_type=F32)


def _dot_nt(a, b):
    return lax.dot_general(a, b, (((1,), (1,)), ((), ())), preferred_element_type=F32)


def _silu(x):
    return x * jax.nn.sigmoid(x)


def _gelu(x):
    return 0.5 * x * (1.0 + lax.erf(x * (0.5 ** 0.5)))


def _proj_kernel(x_ref, g_ref, w_ref, o_ref, h_scr):
    @pl.when(pl.program_id(1) == 0)
    def _():
        x = x_ref[...]
        ms = jnp.mean(x * x, axis=-1, keepdims=True)
        h_scr[...] = (x * lax.rsqrt(ms + EPS) * g_ref[...]).astype(BF16)

    o_ref[...] = _dot(h_scr[...], w_ref[...]).astype(BF16)


def _proj_call(layer, x, g, w):
    n = x.shape[0]
    return pl.pallas_call(
        _proj_kernel,
        grid=(n // PROJ_TM, D_IN // PROJ_TN),
        in_specs=[
            pl.BlockSpec((PROJ_TM, D_MODEL), lambda i, j: (i, 0)),
            pl.BlockSpec((None, 1, D_MODEL), lambda i, j: (layer, 0, 0)),
            pl.BlockSpec((None, D_MODEL, PROJ_TN), lambda i, j: (layer, 0, j)),
        ],
        out_specs=pl.BlockSpec((PROJ_TM, PROJ_TN), lambda i, j: (i, j)),
        out_shape=jax.ShapeDtypeStruct((n, D_IN), BF16),
        scratch_shapes=[pltpu.VMEM((PROJ_TM, D_MODEL), BF16)],
        compiler_params=pltpu.CompilerParams(
            dimension_semantics=("arbitrary", "arbitrary"), vmem_limit_bytes=VMEM_LIMIT),
        name="proj",
    )(x, g, w)


def _memkv_kernel(mem_ref, g_ref, w_ref, o_ref):
    m = mem_ref[...]
    ms = jnp.mean(m * m, axis=-1, keepdims=True)
    mn = (m * lax.rsqrt(ms + EPS) * g_ref[...]).astype(BF16)
    o_ref[...] = _dot(mn, w_ref[...]).astype(BF16)


def _memkv_call(mem, g, w):
    depth = w.shape[0]
    b = mem.shape[0]
    return pl.pallas_call(
        _memkv_kernel,
        grid=(depth, b),
        in_specs=[
            pl.BlockSpec((None, MEM_LEN, D_MODEL), lambda l, i: (i, 0, 0)),
            pl.BlockSpec((None, 1, D_MODEL), lambda l, i: (l, 0, 0)),
            pl.BlockSpec((None, D_MODEL, 2 * WIDTH), lambda l, i: (l, 0, 0)),
        ],
        out_specs=pl.BlockSpec((None, None, MEM_LEN, 2 * WIDTH), lambda l, i: (l, i, 0, 0)),
        out_shape=jax.ShapeDtypeStruct((depth, b, MEM_LEN, 2 * WIDTH), BF16),
        compiler_params=pltpu.CompilerParams(
            dimension_semantics=("arbitrary", "arbitrary"), vmem_limit_bytes=VMEM_LIMIT),
        name="memkv",
    )(mem, g.reshape(depth, 1, D_MODEL), w)


def _att_kernel(q_ref, kc_ref, kp_ref, vc_ref, vp_ref, o_ref, lse_ref, *, tq):
    has_prev = pl.program_id(2) > 0
    row = lax.broadcasted_iota(jnp.int32, (CHUNK, CHUNK), 0)
    col = lax.broadcasted_iota(jnp.int32, (CHUNK, CHUNK), 1)
    cur_mask = col <= row
    prev_mask = col >= row
    scale = HEAD_DIM ** -0.5
    for i in range(tq // CHUNK):
        rows = slice(i * CHUNK, (i + 1) * CHUNK)
        for h in range(HEADS):
            cols = slice(h * HEAD_DIM, (h + 1) * HEAD_DIM)
            q = q_ref[rows, cols]
            kc = kc_ref[rows, cols]
            vc = vc_ref[rows, cols]
            if i == 0:
                kp = kp_ref[:, cols]
                vp = vp_ref[:, cols]
                pmask = jnp.logical_and(prev_mask, has_prev)
            else:
                prows = slice((i - 1) * CHUNK, i * CHUNK)
                kp = kc_ref[prows, cols]
                vp = vc_ref[prows, cols]
                pmask = prev_mask
            sc = jnp.where(cur_mask, _dot_nt(q, kc) * scale, NEG)
            sp = jnp.where(pmask, _dot_nt(q, kp) * scale, NEG)
            m = jnp.maximum(jnp.max(sc, axis=-1, keepdims=True), jnp.max(sp, axis=-1, keepdims=True))
            ec = jnp.exp(sc - m)
            ep = jnp.exp(sp - m)
            den = jnp.sum(ec, axis=-1, keepdims=True) + jnp.sum(ep, axis=-1, keepdims=True)
            o = (_dot(ec.astype(BF16), vc) + _dot(ep.astype(BF16), vp)) / den
            o_ref[rows, cols] = o.astype(BF16)
            lse_ref[rows, cols] = jnp.broadcast_to(m + jnp.log(den), (CHUNK, HEAD_DIM))


def _att_call(proj, batch, seq, group):
    window, dil = DIL_PATTERNS[group]
    assert window // dil == CHUNK
    sub_len = seq // dil
    tq = min(ATT_TQ[dil], sub_len)
    view = proj.reshape(batch, sub_len, dil * D_IN)
    blocks_per_step = tq // CHUNK

    def cur(cb):
        return pl.BlockSpec((None, tq, WIDTH), lambda b, r, n: (b, n, r * N_CB + cb))

    def prev(cb):
        return pl.BlockSpec(
            (None, CHUNK, WIDTH),
            lambda b, r, n: (b, jnp.maximum(n * blocks_per_step - 1, 0), r * N_CB + cb))

    out_spec = pl.BlockSpec((None, tq, WIDTH), lambda b, r, n: (b, n, r))
    o, lse = pl.pallas_call(
        functools.partial(_att_kernel, tq=tq),
        grid=(batch, dil, sub_len // tq),
        in_specs=[cur(CB_CQ + group), cur(CB_CK), prev(CB_CK), cur(CB_CV), prev(CB_CV)],
        out_specs=[out_spec, out_spec],
        out_shape=[jax.ShapeDtypeStruct((batch, sub_len, dil * WIDTH), BF16),
                   jax.ShapeDtypeStruct((batch, sub_len, dil * WIDTH), F32)],
        compiler_params=pltpu.CompilerParams(
            dimension_semantics=("arbitrary", "arbitrary", "arbitrary"), vmem_limit_bytes=VMEM_LIMIT),
        name=f"att_d{dil}",
    )(view, view, view, view, view)
    return o.reshape(batch * seq, WIDTH), lse.reshape(batch * seq, WIDTH)


def _post_kernel(gm_ref, au_ref, av_ref, ag_ref, pi_ref, ph_ref, pg_ref, cg_ref, mq_ref, mg_ref,
                 o1_ref, o2_ref, o3_ref, l1_ref, l2_ref, l3_ref, kvm_ref, x_ref,
                 lng_ref, lnb_ref, ws_ref, bs_ref, pw_ref, ps_ref, wb_ref, wo_ref,
                 out_ref, y_scr, vn_scr, *, t, tiles_per_seq):
    tile_in_seq = pl.program_id(0) % tiles_per_seq

    v = _gelu(av_ref[...].astype(F32))
    mu = jnp.mean(v, axis=-1, keepdims=True)
    vc = v - mu
    var = jnp.mean(vc * vc, axis=-1, keepdims=True)
    vn_scr[...] = (vc * lax.rsqrt(var + EPS) * lng_ref[...] + lnb_ref[...]).astype(BF16)
    row = lax.broadcasted_iota(jnp.int32, (CHUNK, CHUNK), 0)
    col = lax.broadcasted_iota(jnp.int32, (CHUNK, CHUNK), 1)
    for h in range(HEADS):
        cols = slice(h * HEAD_DIM, (h + 1) * HEAD_DIM)
        w_causal = jnp.where(col <= row, ws_ref[h], 0.0).astype(BF16)
        for c in range(t // CHUNK):
            rows = slice(c * CHUNK, (c + 1) * CHUNK)
            mixed = _dot(w_causal, vn_scr[rows, cols]) + bs_ref[h]
            u = _gelu(au_ref[rows, cols].astype(F32))
            y_scr[0, rows, cols] = (u * mixed * _silu(ag_ref[rows, cols].astype(F32))).astype(BF16)

    p = pi_ref[...].astype(F32)
    halo = jnp.where(tile_in_seq == 0, 0.0, ph_ref[...].astype(F32))
    ext = jnp.concatenate([halo, p], axis=0)
    pos = tile_in_seq * t + lax.broadcasted_iota(jnp.int32, (t, 1), 0)
    for g, win in enumerate(POOL_WINDOWS):
        cols = slice(g * HEAD_DIM, (g + 1) * HEAD_DIM)
        acc = ext[:, cols]
        k = 1
        while k < win:
            acc = acc + pltpu.roll(acc, k, axis=0)
            k *= 2
        count = jnp.minimum(pos + 1, win).astype(F32)
        d = acc[POOL_HALO:, :] / count - p[:, cols]
        y = _dot(d.astype(BF16), pw_ref[g]) * ps_ref[:, cols]
        y_scr[1, :, cols] = (y * _silu(pg_ref[:, cols].astype(F32))).astype(BF16)

    l1, l2, l3 = l1_ref[...], l2_ref[...], l3_ref[...]
    lmax = jnp.maximum(jnp.maximum(l1, l2), l3)
    e1, e2, e3 = jnp.exp(l1 - lmax), jnp.exp(l2 - lmax), jnp.exp(l3 - lmax)
    oc = (e1 * o1_ref[...].astype(F32) + e2 * o2_ref[...].astype(F32)
          + e3 * o3_ref[...].astype(F32)) / (e1 + e2 + e3)
    y_scr[2] = (oc * _silu(cg_ref[...].astype(F32))).astype(BF16)

    scale = HEAD_DIM ** -0.5
    for h in range(HEADS):
        cols = slice(h * HEAD_DIM, (h + 1) * HEAD_DIM)
        s = _dot_nt(mq_ref[:, cols], kvm_ref[:, cols]) * scale
        e = jnp.exp(s - jnp.max(s, axis=-1, keepdims=True))
        den = jnp.sum(e, axis=-1, keepdims=True)
        om = _dot(e.astype(BF16), kvm_ref[:, WIDTH + h * HEAD_DIM:WIDTH + (h + 1) * HEAD_DIM]) / den
        y_scr[3, :, cols] = (om * _silu(mg_ref[:, cols].astype(F32))).astype(BF16)

    z = jnp.zeros((t, D_MODEL), F32)
    for b in range(N_BRANCH):
        gate = jax.nn.sigmoid(gm_ref[:, b * D_MODEL:(b + 1) * D_MODEL].astype(F32))
        z = z + gate * _dot(y_scr[b], wb_ref[b])
    out_ref[...] = x_ref[...] + _dot(z.astype(BF16), wo_ref[...])


def _post_call(layer, proj, atts, kvm, x, seq, lng, lnb, ws, bs, pw, ps, wb, wo):
    n = x.shape[0]
    t = POST_T
    tiles_per_seq = seq // t
    halo_blocks_per_tile = t // POOL_HALO

    def pcol(cb, width=WIDTH):
        return pl.BlockSpec((t, width), lambda i: (i, cb * WIDTH // width))

    def tok(width):
        return pl.BlockSpec((t, width), lambda i: (i, 0))

    def layer_blk(*shape):
        return pl.BlockSpec((None,) + shape, lambda i: (layer,) + (0,) * len(shape))

    halo_spec = pl.BlockSpec(
        (POOL_HALO, WIDTH), lambda i: (jnp.maximum(i * halo_blocks_per_tile - 1, 0), CB_PI))
    in_specs = [
        pcol(CB_GATES, N_GATE_COLS), pcol(CB_AU), pcol(CB_AV), pcol(CB_AG), pcol(CB_PI), halo_spec,
        pcol(CB_PG), pcol(CB_CG), pcol(CB_MQ), pcol(CB_MG),
        tok(WIDTH), tok(WIDTH), tok(WIDTH), tok(WIDTH), tok(WIDTH), tok(WIDTH),
        pl.BlockSpec((None, None, MEM_LEN, 2 * WIDTH), lambda i: (layer, i // tiles_per_seq, 0, 0)),
        tok(D_MODEL),
        layer_blk(1, WIDTH), layer_blk(1, WIDTH), layer_blk(HEADS, CHUNK, CHUNK),
        layer_blk(HEADS, CHUNK, CHUNK), layer_blk(len(POOL_WINDOWS), HEAD_DIM, HEAD_DIM),
        layer_blk(1, WIDTH), layer_blk(N_BRANCH, WIDTH, D_MODEL), layer_blk(D_MODEL, D_MODEL),
    ]
    (o1, l1), (o2, l2), (o3, l3) = atts
    return pl.pallas_call(
        functools.partial(_post_kernel, t=t, tiles_per_seq=tiles_per_seq),
        grid=(n // t,),
        in_specs=in_specs,
        out_specs=tok(D_MODEL),
        out_shape=jax.ShapeDtypeStruct((n, D_MODEL), F32),
        scratch_shapes=[pltpu.VMEM((N_BRANCH, t, WIDTH), BF16), pltpu.VMEM((t, WIDTH), BF16)],
        compiler_params=pltpu.CompilerParams(
            dimension_semantics=("arbitrary",), vmem_limit_bytes=VMEM_LIMIT),
        name="post",
    )(proj, proj, proj, proj, proj, proj, proj, proj, proj, proj,
      o1, o2, o3, l1, l2, l3, kvm, x, lng, lnb, ws, bs, pw, ps, wb, wo)


def _final_norm_kernel(x_ref, g_ref, o_ref):
    x = x_ref[...]
    ms = jnp.mean(x * x, axis=-1, keepdims=True)
    o_ref[...] = x * lax.rsqrt(ms + EPS) * g_ref[...]


def _final_norm_call(x, g):
    n = x.shape[0]
    t = PROJ_TM
    return pl.pallas_call(
        _final_norm_kernel,
        grid=(n // t,),
        in_specs=[pl.BlockSpec((t, D_MODEL), lambda i: (i, 0)),
                  pl.BlockSpec((1, D_MODEL), lambda i: (0, 0))],
        out_specs=pl.BlockSpec((t, D_MODEL), lambda i: (i, 0)),
        out_shape=jax.ShapeDtypeStruct((n, D_MODEL), F32),
        compiler_params=pltpu.CompilerParams(
            dimension_semantics=("arbitrary",), vmem_limit_bytes=VMEM_LIMIT),
        name="final_norm",
    )(x, g)


def kernel(x, mem, norm_g, w_in, gm_ln_g, gm_ln_b, gm_ws, gm_bs, pool_w, pool_scale, mem_norm_g,
           w_mem_kv, w_branch, w_out, final_norm_g):
    batch, seq, d_model = x.shape
    depth = w_in.shape[0]
    assert d_model == D_MODEL and w_in.shape[2] == D_IN and mem.shape[1] == MEM_LEN
    assert seq % (DIL_PATTERNS[-1][1] * CHUNK) == 0 and seq % POST_T == 0
    assert (batch * seq) % PROJ_TM == 0
    n = batch * seq

    w_in_r = jnp.concatenate(
        [w_in[:, :, ORIG_GATE_START:], w_in[:, :, :ORIG_GATE_START]], axis=-1).astype(BF16)
    wb = w_branch.astype(BF16)
    wo = w_out.astype(BF16)
    pw = pool_w.astype(BF16)
    bs_tile = jnp.broadcast_to(gm_bs[:, :, :, None], gm_bs.shape + (CHUNK,))
    lng = gm_ln_g.reshape(depth, 1, WIDTH)
    lnb = gm_ln_b.reshape(depth, 1, WIDTH)
    ps = pool_scale.reshape(depth, 1, WIDTH)
    ng = norm_g.reshape(depth, 1, D_MODEL)

    kvm = _memkv_call(mem, mem_norm_g, w_mem_kv.astype(BF16))
    xf = x.reshape(n, D_MODEL)
    for layer in range(depth):
        proj = _proj_call(layer, xf, ng, w_in_r)
        atts = [_att_call(proj, batch, seq, g) for g in range(len(DIL_PATTERNS))]
        xf = _post_call(layer, proj, atts, kvm, xf, seq, lng, lnb, gm_ws, bs_tile, pw, ps, wb, wo)
    out = _final_norm_call(xf, final_norm_g.reshape(1, D_MODEL))
    return out.reshape(batch, seq, D_MODEL)
```

```python
import functools

import jax
import jax.numpy as jnp
from jax import lax
from jax.experimental import pallas as pl
from jax.experimental.pallas import tpu as pltpu

F32 = jnp.float32
BF16 = jnp.bfloat16

EPS = 1e-6
NEG = -1e30
LANES = 128
D_MODEL = 1024
WIDTH = 512
N_BRANCH = 4
HEADS = 4
HEAD_DIM = WIDTH // HEADS
CHUNK = 128
POOL_WINDOWS = (2, 4, 8, 16)
POOL_HALO = 16
DIL_PATTERNS = ((128, 1), (512, 4), (2048, 16))
MEM_LEN = 256

CB_AU, CB_AV, CB_AG = 0, 1, 2
CB_PI, CB_PG = 3, 4
CB_Q1, CB_Q2, CB_Q3, CB_K, CB_V, CB_CG = 5, 6, 7, 8, 9, 10
CB_MQ, CB_MG = 11, 12
CB_GATES = 13
N_CB = 21
D_IN = N_CB * WIDTH
DIL_Q, DIL_K, DIL_V = 0, 1, 2
DIL_COLS = 3 * WIDTH

VMEM_LIMIT = 56 * 1024 * 1024

PROJ_TM = 1024
PROJ_CB_PER_STEP = 3
PROJ_TN = PROJ_CB_PER_STEP * WIDTH
POST_T = 256
ATT_TQ = {1: 512, 4: 512, 16: 256}


def _dot(a, b):
    return jnp.dot(a, b, preferred_element_type=F32)


def _dot_nt(a, b):
    return lax.dot_general(a, b, (((1,), (1,)), ((), ())), preferred_element_type=F32)


def _sigmoid(x):
    return 0.5 * jnp.tanh(0.5 * x) + 0.5


def _silu(x):
    return x * _sigmoid(x)


def _gelu(x):
    return 0.5 * x * (1.0 + lax.erf(x * (0.5 ** 0.5)))


def _deinterleave(slab_scr, slab0, dil, out_ref, out_cb):
    rows = slab_scr.shape[1] // dil
    for s in range(WIDTH // LANES):
        for r in range(dil):
            blk = slab_scr[slab0 + s, pl.ds(r, rows, stride=dil), :]
            out_ref[r, :, out_cb * WIDTH + s * LANES:out_cb * WIDTH + (s + 1) * LANES] = blk.astype(BF16)


def _proj_kernel(x_ref, g_ref, w_ref, o_ref, d4_ref, d16_ref, h_scr, slab_scr):
    j = pl.program_id(1)

    @pl.when(j == 0)
    def _():
        x = x_ref[...]
        ms = jnp.mean(x * x, axis=-1, keepdims=True)
        h_scr[...] = (x * lax.rsqrt(ms + EPS) * g_ref[...]).astype(BF16)

    res = _dot(h_scr[...], w_ref[...])
    o_ref[...] = res.astype(BF16)

    def to_slabs(local_cb, slab0):
        for s in range(WIDTH // LANES):
            c0 = local_cb * WIDTH + s * LANES
            slab_scr[slab0 + s] = res[:, c0:c0 + LANES]

    @pl.when(j == CB_Q2 // PROJ_CB_PER_STEP)
    def _():
        assert (CB_Q2 % PROJ_CB_PER_STEP, CB_Q3 % PROJ_CB_PER_STEP, CB_K % PROJ_CB_PER_STEP) == (0, 1, 2)
        to_slabs(0, 0)
        to_slabs(1, 4)
        to_slabs(2, 8)
        _deinterleave(slab_scr, 0, 4, d4_ref, DIL_Q)
        _deinterleave(slab_scr, 4, 16, d16_ref, DIL_Q)
        _deinterleave(slab_scr, 8, 4, d4_ref, DIL_K)
        _deinterleave(slab_scr, 8, 16, d16_ref, DIL_K)

    @pl.when(j == CB_V // PROJ_CB_PER_STEP)
    def _():
        assert CB_V % PROJ_CB_PER_STEP == 0 and CB_V // PROJ_CB_PER_STEP != CB_Q2 // PROJ_CB_PER_STEP
        to_slabs(0, 0)
        _deinterleave(slab_scr, 0, 4, d4_ref, DIL_V)
        _deinterleave(slab_scr, 0, 16, d16_ref, DIL_V)


def _proj_call(layer, x, g, w, batch, seq):
    n = x.shape[0]
    tiles_per_seq = seq // PROJ_TM

    def dil_spec(dil):
        return pl.BlockSpec((None, dil, PROJ_TM // dil, DIL_COLS),
                            lambda i, j: (i // tiles_per_seq, 0, i % tiles_per_seq, 0))

    def dil_shape(dil):
        return jax.ShapeDtypeStruct((batch, dil, seq // dil, DIL_COLS), BF16)

    return pl.pallas_call(
        _proj_kernel,
        grid=(n // PROJ_TM, D_IN // PROJ_TN),
        in_specs=[
            pl.BlockSpec((PROJ_TM, D_MODEL), lambda i, j: (i, 0)),
            pl.BlockSpec((None, 1, D_MODEL), lambda i, j: (layer, 0, 0)),
            pl.BlockSpec((None, D_MODEL, PROJ_TN), lambda i, j: (layer, 0, j)),
        ],
        out_specs=[pl.BlockSpec((PROJ_TM, PROJ_TN), lambda i, j: (i, j)), dil_spec(4), dil_spec(16)],
        out_shape=[jax.ShapeDtypeStruct((n, D_IN), BF16), dil_shape(4), dil_shape(16)],
        scratch_shapes=[pltpu.VMEM((PROJ_TM, D_MODEL), BF16),
                        pltpu.VMEM((PROJ_CB_PER_STEP * WIDTH // LANES, PROJ_TM, LANES), F32)],
        compiler_params=pltpu.CompilerParams(
            dimension_semantics=("arbitrary", "arbitrary"), vmem_limit_bytes=VMEM_LIMIT),
        name="proj",
    )(x, g, w)


def _memkv_kernel(mem_ref, g_ref, w_ref, o_ref):
    m = mem_ref[...]
    ms = jnp.mean(m * m, axis=-1, keepdims=True)
    mn = (m * lax.rsqrt(ms + EPS) * g_ref[...]).astype(BF16)
    o_ref[...] = _dot(mn, w_ref[...]).astype(BF16)


def _memkv_call(mem, g, w):
    depth = w.shape[0]
    b = mem.shape[0]
    return pl.pallas_call(
        _memkv_kernel,
        grid=(depth, b),
        in_specs=[
            pl.BlockSpec((None, MEM_LEN, D_MODEL), lambda l, i: (i, 0, 0)),
            pl.BlockSpec((None, 1, D_MODEL), lambda l, i: (l, 0, 0)),
            pl.BlockSpec((None, D_MODEL, 2 * WIDTH), lambda l, i: (l, 0, 0)),
        ],
        out_specs=pl.BlockSpec((None, None, MEM_LEN, 2 * WIDTH), lambda l, i: (l, i, 0, 0)),
        out_shape=jax.ShapeDtypeStruct((depth, b, MEM_LEN, 2 * WIDTH), BF16),
        compiler_params=pltpu.CompilerParams(
            dimension_semantics=("arbitrary", "arbitrary"), vmem_limit_bytes=VMEM_LIMIT),
        name="memkv",
    )(mem, g.reshape(depth, 1, D_MODEL), w)


def _att_kernel(q_ref, kc_ref, kp_ref, vc_ref, vp_ref, o_ref, lse_ref, s_scr, m_scr, e_scr, *, tq):
    has_prev = pl.program_id(2) > 0
    row = lax.broadcasted_iota(jnp.int32, (CHUNK, 2 * CHUNK), 0)
    col = lax.broadcasted_iota(jnp.int32, (CHUNK, 2 * CHUNK), 1)
    band = jnp.logical_and(col >= row, col <= row + CHUNK)
    first_band = jnp.logical_and(band, jnp.logical_or(has_prev, col >= CHUNK))
    ones = jnp.ones((2 * CHUNK, HEAD_DIM), BF16)
    scale = HEAD_DIM ** -0.5
    pairs = [(i, h) for i in range(tq // CHUNK) for h in range(HEADS)]

    def key_rows(ref_cur, ref_prev, i, cols):
        if i == 0:
            return jnp.concatenate([ref_prev[:, cols], ref_cur[0:CHUNK, cols]], axis=0)
        return ref_cur[(i - 1) * CHUNK:(i + 1) * CHUNK, cols]

    for idx, (i, h) in enumerate(pairs):
        cols = slice(h * HEAD_DIM, (h + 1) * HEAD_DIM)
        keys = key_rows(kc_ref, kp_ref, i, cols)
        s = _dot_nt(q_ref[i * CHUNK:(i + 1) * CHUNK, cols], keys) * scale
        s = jnp.where(first_band if i == 0 else band, s, NEG)
        s_scr[idx] = s
        m = jnp.max(jnp.maximum(s[:, :CHUNK], s[:, CHUNK:]), axis=-1, keepdims=True)
        m_scr[idx] = jnp.broadcast_to(m, (CHUNK, HEAD_DIM))
    for idx in range(len(pairs)):
        m = m_scr[idx]
        e_scr[idx] = jnp.exp(s_scr[idx] - jnp.concatenate([m, m], axis=1)).astype(BF16)
    for idx, (i, h) in enumerate(pairs):
        cols = slice(h * HEAD_DIM, (h + 1) * HEAD_DIM)
        vals = jnp.concatenate([key_rows(vc_ref, vp_ref, i, cols), ones], axis=1)
        r = _dot(e_scr[idx], vals)
        den = r[:, HEAD_DIM:]
        o_ref[i * CHUNK:(i + 1) * CHUNK, cols] = (r[:, :HEAD_DIM] / den).astype(BF16)
        lse_ref[i * CHUNK:(i + 1) * CHUNK, cols] = m_scr[idx] + jnp.log(den)


def _att_call(src, dil, cq, ck, cv):
    batch, _, sub_len, _ = src.shape
    tq = min(ATT_TQ[dil], sub_len)
    blocks_per_step = tq // CHUNK

    def cur(cb):
        return pl.BlockSpec((None, None, tq, WIDTH), lambda b, r, n: (b, r, n, cb))

    def prev(cb):
        return pl.BlockSpec((None, None, CHUNK, WIDTH),
                            lambda b, r, n: (b, r, jnp.maximum(n * blocks_per_step - 1, 0), cb))

    out_spec = pl.BlockSpec((None, None, tq, WIDTH), lambda b, r, n: (b, r, n, 0))
    return pl.pallas_call(
        functools.partial(_att_kernel, tq=tq),
        grid=(batch, dil, sub_len // tq),
        in_specs=[cur(cq), cur(ck), prev(ck), cur(cv), prev(cv)],
        out_specs=[out_spec, out_spec],
        out_shape=[jax.ShapeDtypeStruct((batch, dil, sub_len, WIDTH), BF16),
                   jax.ShapeDtypeStruct((batch, dil, sub_len, WIDTH), F32)],
        scratch_shapes=[pltpu.VMEM((blocks_per_step * HEADS, CHUNK, 2 * CHUNK), F32),
                        pltpu.VMEM((blocks_per_step * HEADS, CHUNK, HEAD_DIM), F32),
                        pltpu.VMEM((blocks_per_step * HEADS, CHUNK, 2 * CHUNK), BF16)],
        compiler_params=pltpu.CompilerParams(
            dimension_semantics=("arbitrary", "arbitrary", "arbitrary"), vmem_limit_bytes=VMEM_LIMIT),
        name=f"att_d{dil}",
    )(src, src, src, src, src)


def _post_kernel(au_ref, av_ref, ag_ref, pi_ref, ph_ref, pg_ref, cg_ref, mq_ref, mg_ref,
                 g0_ref, g1_ref, g2_ref, g3_ref, g4_ref, g5_ref, g6_ref, g7_ref,
                 o1_ref, l1_ref, o2_ref, l2_ref, o3_ref, l3_ref, kvm_ref, x_ref,
                 lng_ref, lnb_ref, ws_ref, bs_ref, pw_ref, ps_ref, wb_ref, wo_ref,
                 out_ref, y_scr, vn_scr, nat_scr, z_scr, *, t, tiles_per_seq):
    tile_in_seq = pl.program_id(0) % tiles_per_seq
    gate_refs = (g0_ref, g1_ref, g2_ref, g3_ref, g4_ref, g5_ref, g6_ref, g7_ref)

    v = _gelu(av_ref[...].astype(F32))
    mu = jnp.mean(v, axis=-1, keepdims=True)
    vc = v - mu
    var = jnp.mean(vc * vc, axis=-1, keepdims=True)
    vn_scr[...] = (vc * lax.rsqrt(var + EPS) * lng_ref[...] + lnb_ref[...]).astype(BF16)
    row = lax.broadcasted_iota(jnp.int32, (CHUNK, CHUNK), 0)
    col = lax.broadcasted_iota(jnp.int32, (CHUNK, CHUNK), 1)
    for h in range(HEADS):
        cols = slice(h * HEAD_DIM, (h + 1) * HEAD_DIM)
        w_causal = jnp.where(col <= row, ws_ref[h], 0.0).astype(BF16)
        for c in range(t // CHUNK):
            rows = slice(c * CHUNK, (c + 1) * CHUNK)
            mixed = _dot(w_causal, vn_scr[rows, cols]) + bs_ref[h]
            u = _gelu(au_ref[rows, cols].astype(F32))
            y_scr[0, rows, cols] = (u * mixed * _silu(ag_ref[rows, cols].astype(F32))).astype(BF16)

    p = pi_ref[...].astype(F32)
    halo = jnp.where(tile_in_seq == 0, 0.0, ph_ref[...].astype(F32))
    ext = jnp.concatenate([halo, p], axis=0)
    pos = tile_in_seq * t + lax.broadcasted_iota(jnp.int32, (t, 1), 0)
    for g, win in enumerate(POOL_WINDOWS):
        cols = slice(g * HEAD_DIM, (g + 1) * HEAD_DIM)
        acc = ext[:, cols]
        k = 1
        while k < win:
            acc = acc + pltpu.roll(acc, k, axis=0)
            k *= 2
        count = jnp.minimum(pos + 1, win).astype(F32)
        d = acc[POOL_HALO:, :] / count - p[:, cols]
        y = _dot(d.astype(BF16), pw_ref[g]) * ps_ref[:, cols]
        y_scr[1, :, cols] = (y * _silu(pg_ref[:, cols].astype(F32))).astype(BF16)

    for a, (ref, dil) in enumerate(((o2_ref, 4), (l2_ref, 4), (o3_ref, 16), (l3_ref, 16))):
        for h in range(HEADS):
            for r in range(dil):
                nat_scr[a * HEADS + h, pl.ds(r, t // dil, stride=dil), :] = (
                    ref[r, :, h * HEAD_DIM:(h + 1) * HEAD_DIM].astype(F32))
    for h in range(HEADS):
        cols = slice(h * HEAD_DIM, (h + 1) * HEAD_DIM)
        l1, l2, l3 = l1_ref[:, cols], nat_scr[HEADS + h], nat_scr[3 * HEADS + h]
        lmax = jnp.maximum(jnp.maximum(l1, l2), l3)
        e1, e2, e3 = jnp.exp(l1 - lmax), jnp.exp(l2 - lmax), jnp.exp(l3 - lmax)
        oc = (e1 * o1_ref[:, cols].astype(F32) + e2 * nat_scr[h] + e3 * nat_scr[2 * HEADS + h]) / (e1 + e2 + e3)
        y_scr[2, :, cols] = (oc * _silu(cg_ref[:, cols].astype(F32))).astype(BF16)

    scale = HEAD_DIM ** -0.5
    mem_ones = jnp.ones((MEM_LEN, HEAD_DIM), BF16)
    for h in range(HEADS):
        cols = slice(h * HEAD_DIM, (h + 1) * HEAD_DIM)
        s = _dot_nt(mq_ref[:, cols], kvm_ref[:, cols]) * scale
        m = jnp.max(jnp.maximum(s[:, :MEM_LEN // 2], s[:, MEM_LEN // 2:]), axis=-1, keepdims=True)
        e = jnp.exp(s - m).astype(BF16)
        vals = jnp.concatenate(
            [kvm_ref[:, WIDTH + h * HEAD_DIM:WIDTH + (h + 1) * HEAD_DIM], mem_ones], axis=1)
        r = _dot(e, vals)
        om = r[:, :HEAD_DIM] / r[:, HEAD_DIM:]
        y_scr[3, :, cols] = (om * _silu(mg_ref[:, cols].astype(F32))).astype(BF16)

    z = [jnp.zeros((t, WIDTH), F32), jnp.zeros((t, WIDTH), F32)]
    for b in range(N_BRANCH):
        yb = _dot(y_scr[b], wb_ref[b])
        for half in range(2):
            gate = _sigmoid(gate_refs[2 * b + half][...].astype(F32))
            z[half] = z[half] + gate * yb[:, half * WIDTH:(half + 1) * WIDTH]
    z_scr[:, :WIDTH] = z[0].astype(BF16)
    z_scr[:, WIDTH:] = z[1].astype(BF16)
    out_ref[...] = x_ref[...] + _dot(z_scr[...], wo_ref[...])


def _post_call(layer, proj, atts, kvm, x, batch, seq, lng, lnb, ws, bs, pw, ps, wb, wo):
    n = x.shape[0]
    t = POST_T
    tiles_per_seq = seq // t
    halo_blocks_per_tile = t // POOL_HALO

    def pcol(cb):
        return pl.BlockSpec((t, WIDTH), lambda i: (i, cb))

    def layer_blk(*shape):
        return pl.BlockSpec((None,) + shape, lambda i: (layer,) + (0,) * len(shape))

    def dil_blk(dil):
        return pl.BlockSpec((None, dil, t // dil, WIDTH),
                            lambda i: (i // tiles_per_seq, 0, i % tiles_per_seq, 0))

    nat_blk = pl.BlockSpec((None, None, t, WIDTH), lambda i: (i // tiles_per_seq, 0, i % tiles_per_seq, 0))
    halo_spec = pl.BlockSpec(
        (POOL_HALO, WIDTH), lambda i: (jnp.maximum(i * halo_blocks_per_tile - 1, 0), CB_PI))
    in_specs = [
        pcol(CB_AU), pcol(CB_AV), pcol(CB_AG), pcol(CB_PI), halo_spec,
        pcol(CB_PG), pcol(CB_CG), pcol(CB_MQ), pcol(CB_MG),
        *[pcol(CB_GATES + k) for k in range(2 * N_BRANCH)],
        nat_blk, nat_blk, dil_blk(4), dil_blk(4), dil_blk(16), dil_blk(16),
        pl.BlockSpec((None, None, MEM_LEN, 2 * WIDTH), lambda i: (layer, i // tiles_per_seq, 0, 0)),
        pl.BlockSpec((t, D_MODEL), lambda i: (i, 0)),
        layer_blk(1, WIDTH), layer_blk(1, WIDTH), layer_blk(HEADS, CHUNK, CHUNK),
        layer_blk(HEADS, CHUNK, CHUNK), layer_blk(len(POOL_WINDOWS), HEAD_DIM, HEAD_DIM),
        layer_blk(1, WIDTH), layer_blk(N_BRANCH, WIDTH, D_MODEL), layer_blk(D_MODEL, D_MODEL),
    ]
    (o1, l1), (o2, l2), (o3, l3) = atts
    return pl.pallas_call(
        functools.partial(_post_kernel, t=t, tiles_per_seq=tiles_per_seq),
        grid=(n // t,),
        in_specs=in_specs,
        out_specs=pl.BlockSpec((t, D_MODEL), lambda i: (i, 0)),
        out_shape=jax.ShapeDtypeStruct((n, D_MODEL), F32),
        scratch_shapes=[pltpu.VMEM((N_BRANCH, t, WIDTH), BF16), pltpu.VMEM((t, WIDTH), BF16),
                        pltpu.VMEM((4 * HEADS, t, LANES), F32), pltpu.VMEM((t, D_MODEL), BF16)],
        compiler_params=pltpu.CompilerParams(
            dimension_semantics=("arbitrary",), vmem_limit_bytes=VMEM_LIMIT),
        name="post",
    )(*([proj] * (9 + 2 * N_BRANCH)), o1, l1, o2, l2, o3, l3, kvm, x, lng, lnb, ws, bs, pw, ps, wb, wo)


def _final_norm_kernel(x_ref, g_ref, o_ref):
    x = x_ref[...]
    ms = jnp.mean(x * x, axis=-1, keepdims=True)
    o_ref[...] = x * lax.rsqrt(ms + EPS) * g_ref[...]


def _final_norm_call(x, g):
    n = x.shape[0]
    t = PROJ_TM
    return pl.pallas_call(
        _final_norm_kernel,
        grid=(n // t,),
        in_specs=[pl.BlockSpec((t, D_MODEL), lambda i: (i, 0)),
                  pl.BlockSpec((1, D_MODEL), lambda i: (0, 0))],
        out_specs=pl.BlockSpec((t, D_MODEL), lambda i: (i, 0)),
        out_shape=jax.ShapeDtypeStruct((n, D_MODEL), F32),
        compiler_params=pltpu.CompilerParams(
            dimension_semantics=("arbitrary",), vmem_limit_bytes=VMEM_LIMIT),
        name="final_norm",
    )(x, g)


def kernel(x, mem, norm_g, w_in, gm_ln_g, gm_ln_b, gm_ws, gm_bs, pool_w, pool_scale, mem_norm_g,
           w_mem_kv, w_branch, w_out, final_norm_g):
    batch, seq, d_model = x.shape
    depth = w_in.shape[0]
    assert d_model == D_MODEL and w_in.shape[2] == D_IN and mem.shape[1] == MEM_LEN
    assert all(window // dil == CHUNK for window, dil in DIL_PATTERNS)
    assert seq % (DIL_PATTERNS[-1][1] * CHUNK) == 0 and seq % PROJ_TM == 0 and seq % POST_T == 0
    n = batch * seq

    w_in_b = w_in.astype(BF16)
    wb = w_branch.astype(BF16)
    wo = w_out.astype(BF16)
    pw = pool_w.astype(BF16)
    bs_tile = jnp.broadcast_to(gm_bs[:, :, :, None], gm_bs.shape + (CHUNK,))
    lng = gm_ln_g.reshape(depth, 1, WIDTH)
    lnb = gm_ln_b.reshape(depth, 1, WIDTH)
    ps = pool_scale.reshape(depth, 1, WIDTH)
    ng = norm_g.reshape(depth, 1, D_MODEL)

    kvm = _memkv_call(mem, mem_norm_g, w_mem_kv.astype(BF16))
    xf = x.reshape(n, D_MODEL)
    for layer in range(depth):
        proj, dil4, dil16 = _proj_call(layer, xf, ng, w_in_b, batch, seq)
        atts = [_att_call(proj.reshape(batch, 1, seq, D_IN), 1, CB_Q1, CB_K, CB_V),
                _att_call(dil4, 4, DIL_Q, DIL_K, DIL_V),
                _att_call(dil16, 16, DIL_Q, DIL_K, DIL_V)]
        xf = _post_call(layer, proj, atts, kvm, xf, batch, seq, lng, lnb, gm_ws, bs_tile, pw, ps, wb, wo)
    out = _final_norm_call(xf, final_norm_g.reshape(1, D_MODEL))
    return out.reshape(batch, seq, D_MODEL)
```

```python
import functools

import jax
import jax.numpy as jnp
from jax import lax
from jax.experimental import pallas as pl
from jax.experimental.pallas import tpu as pltpu

F32 = jnp.float32
BF16 = jnp.bfloat16

EPS = 1e-6
NEG = -1e30
LANES = 128
D_MODEL = 1024
WIDTH = 512
N_BRANCH = 4
HEADS = 4
HEAD_DIM = WIDTH // HEADS
CHUNK = 128
POOL_WINDOWS = (2, 4, 8, 16)
POOL_HALO = 16
DIL_PATTERNS = ((128, 1), (512, 4), (2048, 16))
MEM_LEN = 256

CB_AU, CB_AV, CB_AG, CB_PI, CB_PG, CB_Q0, CB_Q1, CB_Q2, CB_K, CB_V, CB_CG, CB_MQ, CB_MG, CB_GATES = range(14)
N_CB = CB_GATES + 2 * N_BRANCH
D_IN = N_CB * WIDTH
GROUP_A = (CB_AU, ("gelu", "gelu_ln", "silu"))
GROUP_B = (CB_PI, ("raw", "silu", "raw"))
GROUP_D = (CB_CG, ("silu", "raw", "silu"))
GROUP_E = (CB_GATES, ("sigmoid",) * (2 * N_BRANCH))
HALVED = ("silu", "sigmoid")
KV_K, KV_V = 0, 1
DIL_Q, DIL_K, DIL_V = 0, 1, 2
DIL_COLS = 3 * WIDTH
LSE_LANES_PER_HEAD = LANES // HEADS

VMEM_LIMIT = 56 * 1024 * 1024

PROJ_TM = 1024
PROJ_C_TM = 512
PROJ_E_TN = 2048
POST_T = 512
POST_SUB = 256
ATT_BLOCKING = {1: (1, 512), 4: (1, 512), 16: (4, 256)}


def _dot(a, b):
    return jnp.dot(a, b, preferred_element_type=F32)


def _dot_nt(a, b):
    return lax.dot_general(a, b, (((1,), (1,)), ((), ())), preferred_element_type=F32)


def _gelu(x):
    return 0.5 * x * (1.0 + lax.erf(x * (0.5 ** 0.5)))


def _rms_norm(x, g):
    ms = jnp.mean(x * x, axis=-1, keepdims=True)
    return x * lax.rsqrt(ms + EPS) * g


def _activate(res, mode, lng_ref, lnb_ref):
    if mode == "raw":
        return res
    if mode == "silu":
        return res * jnp.tanh(res) + res
    if mode == "sigmoid":
        return 0.5 * jnp.tanh(res) + 0.5
    if mode == "gelu":
        return _gelu(res)
    assert mode == "gelu_ln"
    v = _gelu(res)
    vc = v - jnp.mean(v, axis=-1, keepdims=True)
    var = jnp.mean(vc * vc, axis=-1, keepdims=True)
    return vc * lax.rsqrt(var + EPS) * lng_ref[...] + lnb_ref[...]


def _proj_kernel(*refs, modes, from_x):
    if from_x:
        x_ref, g_ref, w_ref, lng_ref, lnb_ref, h_out_ref, o_ref = refs
        h = _rms_norm(x_ref[...], g_ref[...]).astype(BF16)
        h_out_ref[...] = h
    else:
        h_ref, w_ref, lng_ref, lnb_ref, o_ref = refs
        h = h_ref[...]
    res = _dot(h, w_ref[...])
    for k, mode in enumerate(modes):
        cols = slice(k * WIDTH, (k + 1) * WIDTH)
        o_ref[:, cols] = _activate(res[:, cols], mode, lng_ref, lnb_ref).astype(BF16)


def _proj_call(layer, group, src, ng, w, lng, lnb, from_x=False, tn=None):
    _, modes = group
    n = src.shape[0]
    ncols = len(modes) * WIDTH
    tn = ncols if tn is None else tn
    step_modes = modes[:tn // WIDTH]
    assert all(modes[k:k + len(step_modes)] == step_modes for k in range(0, len(modes), len(step_modes)))

    def layer_vec(width):
        return pl.BlockSpec((None, 1, width), lambda j, i: (layer, 0, 0))

    in_specs = [pl.BlockSpec((PROJ_TM, D_MODEL), lambda j, i: (i, 0))]
    args = [src]
    if from_x:
        in_specs.append(layer_vec(D_MODEL))
        args.append(ng)
    in_specs += [pl.BlockSpec((None, D_MODEL, tn), lambda j, i: (layer, 0, j)), layer_vec(WIDTH), layer_vec(WIDTH)]
    args += [w, lng, lnb]
    out_specs = [pl.BlockSpec((PROJ_TM, tn), lambda j, i: (i, j))]
    out_shape = [jax.ShapeDtypeStruct((n, ncols), BF16)]
    if from_x:
        assert tn == ncols
        out_specs.insert(0, pl.BlockSpec((PROJ_TM, D_MODEL), lambda j, i: (i, 0)))
        out_shape.insert(0, jax.ShapeDtypeStruct((n, D_MODEL), BF16))
    return pl.pallas_call(
        functools.partial(_proj_kernel, modes=step_modes, from_x=from_x),
        grid=(ncols // tn, n // PROJ_TM),
        in_specs=in_specs,
        out_specs=out_specs,
        out_shape=out_shape,
        compiler_params=pltpu.CompilerParams(
            dimension_semantics=("arbitrary", "arbitrary"), vmem_limit_bytes=VMEM_LIMIT),
        name="proj_" + "_".join(sorted(set(modes))),
    )(*args)


def _proj_c_kernel(h_ref, w_ref, kv_ref, d4_ref, d16_ref, slab_scr, slab4_scr):
    tm = h_ref.shape[0]
    res = _dot(h_ref[...], w_ref[...])
    kv_ref[...] = res[:, 2 * WIDTH:].astype(BF16)
    slabs_per_block = WIDTH // LANES
    plan = ((0, DIL_Q, None), (1, None, DIL_Q), (2, DIL_K, DIL_K), (3, DIL_V, DIL_V))
    for src_cb, d4_cb, d16_cb in plan:
        for s in range(slabs_per_block):
            slab = src_cb * slabs_per_block + s
            c0 = src_cb * WIDTH + s * LANES
            slab_scr[slab] = res[:, c0:c0 + LANES]
            for r4 in range(4):
                by4 = slab_scr[slab, pl.ds(r4, tm // 4, stride=4), :]
                if d4_cb is not None:
                    d4_ref[r4, :, d4_cb * WIDTH + s * LANES:d4_cb * WIDTH + (s + 1) * LANES] = by4.astype(BF16)
                if d16_cb is not None:
                    slab4_scr[slab, r4 * (tm // 4):(r4 + 1) * (tm // 4), :] = by4
            if d16_cb is not None:
                for r4 in range(4):
                    for r in range(4):
                        by16 = slab4_scr[slab, pl.ds(r4 * (tm // 4) + r, tm // 16, stride=4), :]
                        d16_ref[r4 + 4 * r, :, d16_cb * WIDTH + s * LANES:d16_cb * WIDTH + (s + 1) * LANES] = (
                            by16.astype(BF16))


def _proj_c_call(layer, h, w, batch, seq):
    n = h.shape[0]
    tm = PROJ_C_TM
    tiles_per_seq = seq // tm
    n_slabs = 4 * WIDTH // LANES

    def dil_spec(dil):
        return pl.BlockSpec((None, dil, tm // dil, DIL_COLS),
                            lambda i: (i // tiles_per_seq, 0, i % tiles_per_seq, 0))

    def dil_shape(dil):
        return jax.ShapeDtypeStruct((batch, dil, seq // dil, DIL_COLS), BF16)

    return pl.pallas_call(
        _proj_c_kernel,
        grid=(n // tm,),
        in_specs=[pl.BlockSpec((tm, D_MODEL), lambda i: (i, 0)),
                  pl.BlockSpec((None, D_MODEL, 4 * WIDTH), lambda i: (layer, 0, 0))],
        out_specs=[pl.BlockSpec((tm, 2 * WIDTH), lambda i: (i, 0)), dil_spec(4), dil_spec(16)],
        out_shape=[jax.ShapeDtypeStruct((n, 2 * WIDTH), BF16), dil_shape(4), dil_shape(16)],
        scratch_shapes=[pltpu.VMEM((n_slabs, tm, LANES), F32), pltpu.VMEM((n_slabs, tm, LANES), F32)],
        compiler_params=pltpu.CompilerParams(
            dimension_semantics=("arbitrary",), vmem_limit_bytes=VMEM_LIMIT),
        name="proj_qkv",
    )(h, w)


def _memkv_kernel(mem_ref, g_ref, w_ref, o_ref):
    mn = _rms_norm(mem_ref[...], g_ref[...]).astype(BF16)
    o_ref[...] = _dot(mn, w_ref[...]).astype(BF16)


def _memkv_call(mem, g, w):
    depth = w.shape[0]
    b = mem.shape[0]
    return pl.pallas_call(
        _memkv_kernel,
        grid=(depth, b),
        in_specs=[
            pl.BlockSpec((None, MEM_LEN, D_MODEL), lambda l, i: (i, 0, 0)),
            pl.BlockSpec((None, 1, D_MODEL), lambda l, i: (l, 0, 0)),
            pl.BlockSpec((None, D_MODEL, 2 * WIDTH), lambda l, i: (l, 0, 0)),
        ],
        out_specs=pl.BlockSpec((None, None, MEM_LEN, 2 * WIDTH), lambda l, i: (l, i, 0, 0)),
        out_shape=jax.ShapeDtypeStruct((depth, b, MEM_LEN, 2 * WIDTH), BF16),
        compiler_params=pltpu.CompilerParams(
            dimension_semantics=("arbitrary", "arbitrary"), vmem_limit_bytes=VMEM_LIMIT),
        name="memkv",
    )(mem, g.reshape(depth, 1, D_MODEL), w)


def _att_kernel(q_ref, kc_ref, kp_ref, vc_ref, vp_ref, o_ref, lse_ref, s_scr, m_scr, e_scr, *, nres, tq):
    has_prev = pl.program_id(2) > 0
    row = lax.broadcasted_iota(jnp.int32, (CHUNK, 2 * CHUNK), 0)
    col = lax.broadcasted_iota(jnp.int32, (CHUNK, 2 * CHUNK), 1)
    band = jnp.logical_and(col >= row, col <= row + CHUNK)
    first_band = jnp.logical_and(band, jnp.logical_or(has_prev, col >= CHUNK))
    lane_head = lax.broadcasted_iota(jnp.int32, (CHUNK, LANES), 1) // LSE_LANES_PER_HEAD
    ones = jnp.ones((2 * CHUNK, HEAD_DIM), BF16)
    scale = HEAD_DIM ** -0.5
    blocks = [(rr, i) for rr in range(nres) for i in range(tq // CHUNK)]

    def key_rows(ref_cur, ref_prev, rr, i, cols):
        if i == 0:
            return jnp.concatenate([ref_prev[rr, :, cols], ref_cur[rr, 0:CHUNK, cols]], axis=0)
        return ref_cur[rr, (i - 1) * CHUNK:(i + 1) * CHUNK, cols]

    for bi, (rr, i) in enumerate(blocks):
        for h in range(HEADS):
            cols = slice(h * HEAD_DIM, (h + 1) * HEAD_DIM)
            keys = key_rows(kc_ref, kp_ref, rr, i, cols)
            s = _dot_nt(q_ref[rr, i * CHUNK:(i + 1) * CHUNK, cols], keys) * scale
            s = jnp.where(first_band if i == 0 else band, s, NEG)
            s_scr[bi * HEADS + h] = s
            m = jnp.max(jnp.maximum(s[:, :CHUNK], s[:, CHUNK:]), axis=-1, keepdims=True)
            m_scr[bi * HEADS + h] = jnp.broadcast_to(m, (CHUNK, HEAD_DIM))
    for idx in range(len(blocks) * HEADS):
        m = m_scr[idx]
        e_scr[idx] = jnp.exp(s_scr[idx] - jnp.concatenate([m, m], axis=1)).astype(BF16)
    for bi, (rr, i) in enumerate(blocks):
        rows = slice(i * CHUNK, (i + 1) * CHUNK)
        lse = None
        for h in range(HEADS):
            cols = slice(h * HEAD_DIM, (h + 1) * HEAD_DIM)
            vals = jnp.concatenate([key_rows(vc_ref, vp_ref, rr, i, cols), ones], axis=1)
            r = _dot(e_scr[bi * HEADS + h], vals)
            den = r[:, HEAD_DIM:]
            o_ref[rr, rows, cols] = (r[:, :HEAD_DIM] / den).astype(BF16)
            lse_h = m_scr[bi * HEADS + h] + jnp.log(den)
            lse = lse_h if h == 0 else jnp.where(lane_head == h, lse_h, lse)
        lse_ref[rr, rows, :] = lse


def _att_call(dil, q_src, cq, k_src, ck, v_src, cv):
    batch, _, sub_len, _ = q_src.shape
    nres, tq = ATT_BLOCKING[dil]
    tq = min(tq, sub_len)
    blocks_per_step = tq // CHUNK
    n_triples = nres * blocks_per_step * HEADS

    def cur(cb):
        return pl.BlockSpec((None, nres, tq, WIDTH), lambda b, r, n: (b, r, n, cb))

    def prev(cb):
        return pl.BlockSpec((None, nres, CHUNK, WIDTH),
                            lambda b, r, n: (b, r, jnp.maximum(n * blocks_per_step - 1, 0), cb))

    return pl.pallas_call(
        functools.partial(_att_kernel, nres=nres, tq=tq),
        grid=(batch, dil // nres, sub_len // tq),
        in_specs=[cur(cq), cur(ck), prev(ck), cur(cv), prev(cv)],
        out_specs=[pl.BlockSpec((None, nres, tq, WIDTH), lambda b, r, n: (b, r, n, 0)),
                   pl.BlockSpec((None, nres, tq, LANES), lambda b, r, n: (b, r, n, 0))],
        out_shape=[jax.ShapeDtypeStruct((batch, dil, sub_len, WIDTH), BF16),
                   jax.ShapeDtypeStruct((batch, dil, sub_len, LANES), F32)],
        scratch_shapes=[pltpu.VMEM((n_triples, CHUNK, 2 * CHUNK), F32),
                        pltpu.VMEM((n_triples, CHUNK, HEAD_DIM), F32),
                        pltpu.VMEM((n_triples, CHUNK, 2 * CHUNK), BF16)],
        compiler_params=pltpu.CompilerParams(
            dimension_semantics=("arbitrary", "arbitrary", "arbitrary"), vmem_limit_bytes=VMEM_LIMIT),
        name=f"att_d{dil}",
    )(q_src, k_src, k_src, v_src, v_src)


def _post_kernel(u_ref, vn_ref, sa_ref, pi_ref, ph_ref, sp_ref, sc_ref, mq_ref, sm_ref, gate_ref,
                 o1_ref, l1_ref, o2_ref, l2_ref, o3_ref, l3_ref, kvm_ref, x_ref,
                 ws_ref, bs_ref, pw_ref, ps_ref, wb_ref, wo_ref, fg_ref,
                 out_ref, y_scr, nat_scr, lse_scr, z_scr, *, t, tiles_per_seq, final_norm):
    tile_in_seq = pl.program_id(0) % tiles_per_seq
    row = lax.broadcasted_iota(jnp.int32, (CHUNK, CHUNK), 0)
    col = lax.broadcasted_iota(jnp.int32, (CHUNK, CHUNK), 1)
    w_causal = [jnp.where(col <= row, ws_ref[h], 0.0).astype(BF16) for h in range(HEADS)]
    sel_row = lax.broadcasted_iota(jnp.int32, (LANES, WIDTH), 0)
    sel_col = lax.broadcasted_iota(jnp.int32, (LANES, WIDTH), 1)
    sel = (sel_row == (sel_col // HEAD_DIM) * LSE_LANES_PER_HEAD).astype(BF16)
    mem_ones = jnp.ones((MEM_LEN, HEAD_DIM), BF16)
    scale = HEAD_DIM ** -0.5
    sub = POST_SUB

    for si in range(t // sub):
        r0 = si * sub
        rs = slice(r0, r0 + sub)

        for h in range(HEADS):
            cols = slice(h * HEAD_DIM, (h + 1) * HEAD_DIM)
            for c in range(sub // CHUNK):
                rows = slice(r0 + c * CHUNK, r0 + (c + 1) * CHUNK)
                mixed = _dot(w_causal[h], vn_ref[rows, cols]) + bs_ref[h]
                y_scr[si, 0, c * CHUNK:(c + 1) * CHUNK, cols] = (
                    u_ref[rows, cols].astype(F32) * mixed * sa_ref[rows, cols].astype(F32)).astype(BF16)

        p = pi_ref[rs, :].astype(F32)
        if si == 0:
            halo = jnp.where(tile_in_seq == 0, 0.0, ph_ref[...].astype(F32))
        else:
            halo = pi_ref[r0 - POOL_HALO:r0, :].astype(F32)
        ext = jnp.concatenate([halo, p], axis=0)
        pos = tile_in_seq * t + r0 + lax.broadcasted_iota(jnp.int32, (sub, 1), 0)
        for g, win in enumerate(POOL_WINDOWS):
            cols = slice(g * HEAD_DIM, (g + 1) * HEAD_DIM)
            acc = ext[:, cols]
            k = 1
            while k < win:
                acc = acc + pltpu.roll(acc, k, axis=0)
                k *= 2
            count = jnp.minimum(pos + 1, win).astype(F32)
            d = acc[POOL_HALO:, :] / count - p[:, cols]
            y = _dot(d.astype(BF16), pw_ref[g]) * ps_ref[:, cols]
            y_scr[si, 1, :, cols] = (y * sp_ref[rs, cols].astype(F32)).astype(BF16)

        for a, (o_ref, l_ref, dil) in enumerate(((o2_ref, l2_ref, 4), (o3_ref, l3_ref, 16))):
            drows = slice(r0 // dil, (r0 + sub) // dil)
            for r in range(dil):
                lse_scr[si, a, pl.ds(r, sub // dil, stride=dil), :] = l_ref[r, drows, :]
                for h in range(HEADS):
                    nat_scr[si, a * HEADS + h, pl.ds(r, sub // dil, stride=dil), :] = (
                        o_ref[r, drows, h * HEAD_DIM:(h + 1) * HEAD_DIM].astype(F32))
        l1, l2, l3 = l1_ref[rs, :], lse_scr[si, 0], lse_scr[si, 1]
        lmax = jnp.maximum(jnp.maximum(l1, l2), l3)
        e1, e2, e3 = jnp.exp(l1 - lmax), jnp.exp(l2 - lmax), jnp.exp(l3 - lmax)
        inv = 1.0 / (e1 + e2 + e3)
        w1, w2, w3 = (_dot((e * inv).astype(BF16), sel) for e in (e1, e2, e3))
        for h in range(HEADS):
            cols = slice(h * HEAD_DIM, (h + 1) * HEAD_DIM)
            oc = (w1[:, cols] * o1_ref[rs, cols].astype(F32) + w2[:, cols] * nat_scr[si, h]
                  + w3[:, cols] * nat_scr[si, HEADS + h])
            y_scr[si, 2, :, cols] = (oc * sc_ref[rs, cols].astype(F32)).astype(BF16)

        for h in range(HEADS):
            cols = slice(h * HEAD_DIM, (h + 1) * HEAD_DIM)
            s = _dot_nt(mq_ref[rs, cols], kvm_ref[:, cols]) * scale
            m = jnp.max(jnp.maximum(s[:, :MEM_LEN // 2], s[:, MEM_LEN // 2:]), axis=-1, keepdims=True)
            e = jnp.exp(s - m).astype(BF16)
            vals = jnp.concatenate(
                [kvm_ref[:, WIDTH + h * HEAD_DIM:WIDTH + (h + 1) * HEAD_DIM], mem_ones], axis=1)
            r = _dot(e, vals)
            om = r[:, :HEAD_DIM] / r[:, HEAD_DIM:]
            y_scr[si, 3, :, cols] = (om * sm_ref[rs, cols].astype(F32)).astype(BF16)

        z = [jnp.zeros((sub, WIDTH), F32), jnp.zeros((sub, WIDTH), F32)]
        for b in range(N_BRANCH):
            yb = _dot(y_scr[si, b], wb_ref[b])
            for half in range(2):
                c0 = b * D_MODEL + half * WIDTH
                z[half] = z[half] + (gate_ref[rs, c0:c0 + WIDTH].astype(F32)
                                     * yb[:, half * WIDTH:(half + 1) * WIDTH])
        z_scr[si, :, :WIDTH] = z[0].astype(BF16)
        z_scr[si, :, WIDTH:] = z[1].astype(BF16)
        x_new = x_ref[rs, :] + _dot(z_scr[si], wo_ref[...])
        out_ref[rs, :] = _rms_norm(x_new, fg_ref[...]) if final_norm else x_new


def _post_call(layer, pa, pb, pd, gates, atts, kvm, x, batch, seq, ws, bs, pw, ps, wb, wo, fg, final_norm):
    n = x.shape[0]
    t = POST_T
    n_sub = t // POST_SUB
    tiles_per_seq = seq // t
    halo_blocks_per_tile = t // POOL_HALO

    def col_blk(cb):
        return pl.BlockSpec((t, WIDTH), lambda i: (i, cb))

    def layer_blk(*shape):
        return pl.BlockSpec((None,) + shape, lambda i: (layer,) + (0,) * len(shape))

    def dil_blk(dil, width):
        return pl.BlockSpec((None, dil, t // dil, width),
                            lambda i: (i // tiles_per_seq, 0, i % tiles_per_seq, 0))

    halo_spec = pl.BlockSpec((POOL_HALO, WIDTH), lambda i: (jnp.maximum(i * halo_blocks_per_tile - 1, 0), 0))
    (o1, l1), (o2, l2), (o3, l3) = atts
    operands = [
        (pa, col_blk(0)), (pa, col_blk(1)), (pa, col_blk(2)),
        (pb, col_blk(0)), (pb, halo_spec), (pb, col_blk(1)),
        (pd, col_blk(0)), (pd, col_blk(1)), (pd, col_blk(2)),
        (gates, pl.BlockSpec((t, N_BRANCH * D_MODEL), lambda i: (i, 0))),
        (o1.reshape(n, WIDTH), col_blk(0)), (l1.reshape(n, LANES), pl.BlockSpec((t, LANES), lambda i: (i, 0))),
        (o2, dil_blk(4, WIDTH)), (l2, dil_blk(4, LANES)), (o3, dil_blk(16, WIDTH)), (l3, dil_blk(16, LANES)),
        (kvm, pl.BlockSpec((None, None, MEM_LEN, 2 * WIDTH), lambda i: (layer, i // tiles_per_seq, 0, 0))),
        (x, pl.BlockSpec((t, D_MODEL), lambda i: (i, 0))),
        (ws, layer_blk(HEADS, CHUNK, CHUNK)), (bs, layer_blk(HEADS, CHUNK, CHUNK)),
        (pw, layer_blk(len(POOL_WINDOWS), HEAD_DIM, HEAD_DIM)), (ps, layer_blk(1, WIDTH)),
        (wb, layer_blk(N_BRANCH, WIDTH, D_MODEL)), (wo, layer_blk(D_MODEL, D_MODEL)),
        (fg, pl.BlockSpec((1, D_MODEL), lambda i: (0, 0))),
    ]
    return pl.pallas_call(
        functools.partial(_post_kernel, t=t, tiles_per_seq=tiles_per_seq, final_norm=final_norm),
        grid=(n // t,),
        in_specs=[spec for _, spec in operands],
        out_specs=pl.BlockSpec((t, D_MODEL), lambda i: (i, 0)),
        out_shape=jax.ShapeDtypeStruct((n, D_MODEL), F32),
        scratch_shapes=[pltpu.VMEM((n_sub, N_BRANCH, POST_SUB, WIDTH), BF16),
                        pltpu.VMEM((n_sub, 2 * HEADS, POST_SUB, LANES), F32),
                        pltpu.VMEM((n_sub, 2, POST_SUB, LANES), F32),
                        pltpu.VMEM((n_sub, POST_SUB, D_MODEL), BF16)],
        compiler_params=pltpu.CompilerParams(
            dimension_semantics=("arbitrary",), vmem_limit_bytes=VMEM_LIMIT),
        name="post",
    )(*[arr for arr, _ in operands])


def _group_weights(w_in, group):
    cb0, modes = group
    scale = jnp.repeat(jnp.array([0.5 if m in HALVED else 1.0 for m in modes], F32), WIDTH)
    return (w_in[:, :, cb0 * WIDTH:(cb0 + len(modes)) * WIDTH] * scale).astype(BF16)


def kernel(x, mem, norm_g, w_in, gm_ln_g, gm_ln_b, gm_ws, gm_bs, pool_w, pool_scale, mem_norm_g,
           w_mem_kv, w_branch, w_out, final_norm_g):
    batch, seq, d_model = x.shape
    depth = w_in.shape[0]
    assert d_model == D_MODEL and w_in.shape[2] == D_IN and mem.shape[1] == MEM_LEN
    assert all(window // dil == CHUNK for window, dil in DIL_PATTERNS)
    assert seq % (DIL_PATTERNS[-1][1] * CHUNK) == 0 and seq % PROJ_TM == 0 and seq % POST_T == 0
    n = batch * seq

    w_a, w_b, w_d, w_e = (_group_weights(w_in, g) for g in (GROUP_A, GROUP_B, GROUP_D, GROUP_E))
    w_c = w_in[:, :, CB_Q1 * WIDTH:(CB_V + 1) * WIDTH].astype(BF16)
    wb = w_branch.astype(BF16)
    wo = w_out.astype(BF16)
    pw = pool_w.astype(BF16)
    bs_tile = jnp.broadcast_to(gm_bs[:, :, :, None], gm_bs.shape + (CHUNK,))
    lng = gm_ln_g.reshape(depth, 1, WIDTH)
    lnb = gm_ln_b.reshape(depth, 1, WIDTH)
    ps = pool_scale.reshape(depth, 1, WIDTH)
    ng = norm_g.reshape(depth, 1, D_MODEL)
    fg = final_norm_g.reshape(1, D_MODEL)

    kvm = _memkv_call(mem, mem_norm_g, w_mem_kv.astype(BF16))
    xf = x.reshape(n, D_MODEL)
    for layer in range(depth):
        h, pa = _proj_call(layer, GROUP_A, xf, ng, w_a, lng, lnb, from_x=True)
        (pb,) = _proj_call(layer, GROUP_B, h, ng, w_b, lng, lnb)
        (pd,) = _proj_call(layer, GROUP_D, h, ng, w_d, lng, lnb)
        (gates,) = _proj_call(layer, GROUP_E, h, ng, w_e, lng, lnb, tn=PROJ_E_TN)
        kv, dil4, dil16 = _proj_c_call(layer, h, w_c, batch, seq)
        atts = [_att_call(1, pb.reshape(batch, 1, seq, 3 * WIDTH), 2,
                          kv.reshape(batch, 1, seq, 2 * WIDTH), KV_K, kv.reshape(batch, 1, seq, 2 * WIDTH), KV_V),
                _att_call(4, dil4, DIL_Q, dil4, DIL_K, dil4, DIL_V),
                _att_call(16, dil16, DIL_Q, dil16, DIL_K, dil16, DIL_V)]
        xf = _post_call(layer, pa, pb, pd, gates, atts, kvm, xf, batch, seq,
                        gm_ws, bs_tile, pw, ps, wb, wo, fg, final_norm=(layer == depth - 1))
    return xf.reshape(batch, seq, D_MODEL)
```

```python
import functools

import jax
import jax.numpy as jnp
from jax import lax
from jax.experimental import pallas as pl
from jax.experimental.pallas import tpu as pltpu

F32 = jnp.float32
BF16 = jnp.bfloat16

EPS = 1e-6
NEG = -1e30
LANES = 128
D_MODEL = 1024
WIDTH = 512
N_BRANCH = 4
HEADS = 4
HEAD_DIM = WIDTH // HEADS
CHUNK = 128
POOL_WINDOWS = (2, 4, 8, 16)
POOL_HALO = 16
DIL_PATTERNS = ((128, 1), (512, 4), (2048, 16))
MEM_LEN = 256

CB_AU, CB_AV, CB_AG, CB_PI, CB_PG, CB_Q0, CB_Q1, CB_Q2, CB_K, CB_V, CB_CG, CB_MQ, CB_MG, CB_GATES = range(14)
N_CB = CB_GATES + 2 * N_BRANCH
D_IN = N_CB * WIDTH
GROUP_A = (CB_AU, ("gelu", "gelu_ln", "silu"))
GROUP_B = (CB_PI, ("raw", "silu", "raw"))
GROUP_D = (CB_CG, ("silu", "raw", "silu"))
GROUP_E = (CB_GATES, ("sigmoid",) * (2 * N_BRANCH))
HALVED = ("silu", "sigmoid")
KV_K, KV_V = 0, 1
DIL_Q, DIL_K, DIL_V = 0, 1, 2
DIL_COLS = 3 * WIDTH
LSE_LANES_PER_HEAD = LANES // HEADS

VMEM_LIMIT = 56 * 1024 * 1024

PROJ_TM = 1024
PROJ_C_TM = 1024
PROJ_E_TN = 2048
POST_T = 512
POST_SUB = 256
ATT_BLOCKING = {1: (1, 1024), 4: (2, 512), 16: (4, 256)}


def _dot(a, b):
    return jnp.dot(a, b, preferred_element_type=F32)


def _dot_nt(a, b):
    return lax.dot_general(a, b, (((1,), (1,)), ((), ())), preferred_element_type=F32)


def _gelu(x):
    return 0.5 * x * (1.0 + lax.erf(x * (0.5 ** 0.5)))


def _rms_norm(x, g):
    ms = jnp.mean(x * x, axis=-1, keepdims=True)
    return x * lax.rsqrt(ms + EPS) * g


def _activate(res, mode, lng_ref, lnb_ref):
    if mode == "raw":
        return res
    if mode == "silu":
        return res * jnp.tanh(res) + res
    if mode == "sigmoid":
        return 0.5 * jnp.tanh(res) + 0.5
    if mode == "gelu":
        return _gelu(res)
    assert mode == "gelu_ln"
    v = _gelu(res)
    vc = v - jnp.mean(v, axis=-1, keepdims=True)
    var = jnp.mean(vc * vc, axis=-1, keepdims=True)
    return vc * lax.rsqrt(var + EPS) * lng_ref[...] + lnb_ref[...]


def _cast_weights(w_refs, modes, w_scr):
    for k, mode in enumerate(modes):
        w = w_refs[k][...]
        w_scr[:, k * WIDTH:(k + 1) * WIDTH] = (w * 0.5 if mode in HALVED else w).astype(BF16)


def _proj_kernel(*refs, modes, from_x):
    n_src = 2 if from_x else 1
    w_refs = refs[n_src:n_src + len(modes)]
    lng_ref, lnb_ref = refs[n_src + len(modes):n_src + len(modes) + 2]
    o_ref, w_scr = refs[-2:]

    @pl.when(pl.program_id(1) == 0)
    def _():
        _cast_weights(w_refs, modes, w_scr)

    if from_x:
        x_ref, g_ref = refs[:2]
        h_out_ref = refs[-3]
        h = _rms_norm(x_ref[...], g_ref[...]).astype(BF16)
        h_out_ref[...] = h
    else:
        h = refs[0][...]
    res = _dot(h, w_scr[...])
    for k, mode in enumerate(modes):
        cols = slice(k * WIDTH, (k + 1) * WIDTH)
        o_ref[:, cols] = _activate(res[:, cols], mode, lng_ref, lnb_ref).astype(BF16)


def _proj_call(layer, group, src, ng, w_in, lng, lnb, from_x=False, tn=None):
    cb0, modes = group
    n = src.shape[0]
    ncols = len(modes) * WIDTH
    tn = ncols if tn is None else tn
    step_modes = modes[:tn // WIDTH]
    assert all(modes[k:k + len(step_modes)] == step_modes for k in range(0, len(modes), len(step_modes)))

    def layer_vec(width):
        return pl.BlockSpec((None, 1, width), lambda j, i: (layer, 0, 0))

    def w_spec(k):
        return pl.BlockSpec((None, D_MODEL, WIDTH), lambda j, i: (layer, 0, cb0 + j * len(step_modes) + k))

    in_specs = [pl.BlockSpec((PROJ_TM, D_MODEL), lambda j, i: (i, 0))]
    args = [src]
    if from_x:
        in_specs.append(layer_vec(D_MODEL))
        args.append(ng)
    in_specs += [w_spec(k) for k in range(len(step_modes))] + [layer_vec(WIDTH), layer_vec(WIDTH)]
    args += [w_in] * len(step_modes) + [lng, lnb]
    out_specs = [pl.BlockSpec((PROJ_TM, tn), lambda j, i: (i, j))]
    out_shape = [jax.ShapeDtypeStruct((n, ncols), BF16)]
    if from_x:
        assert tn == ncols
        out_specs.insert(0, pl.BlockSpec((PROJ_TM, D_MODEL), lambda j, i: (i, 0)))
        out_shape.insert(0, jax.ShapeDtypeStruct((n, D_MODEL), BF16))
    return pl.pallas_call(
        functools.partial(_proj_kernel, modes=step_modes, from_x=from_x),
        grid=(ncols // tn, n // PROJ_TM),
        in_specs=in_specs,
        out_specs=out_specs,
        out_shape=out_shape,
        scratch_shapes=[pltpu.VMEM((D_MODEL, tn), BF16)],
        compiler_params=pltpu.CompilerParams(
            dimension_semantics=("arbitrary", "arbitrary"), vmem_limit_bytes=VMEM_LIMIT),
        name="proj_" + "_".join(sorted(set(modes))),
    )(*args)


def _proj_c_kernel(h_ref, w0_ref, w1_ref, w2_ref, w3_ref, kv_ref, d4_ref, d16_ref, w_scr, slab_scr, slab4_scr):
    @pl.when(pl.program_id(0) == 0)
    def _():
        _cast_weights((w0_ref, w1_ref, w2_ref, w3_ref), ("raw",) * 4, w_scr)

    tm = h_ref.shape[0]
    h = h_ref[...]
    slabs_per_block = WIDTH // LANES
    plan = ((0, None, DIL_Q, None), (1, None, None, DIL_Q), (2, KV_K, DIL_K, DIL_K), (3, KV_V, DIL_V, DIL_V))
    for src_cb, kv_cb, d4_cb, d16_cb in plan:
        res = _dot(h, w_scr[:, src_cb * WIDTH:(src_cb + 1) * WIDTH])
        if kv_cb is not None:
            kv_ref[:, kv_cb * WIDTH:(kv_cb + 1) * WIDTH] = res.astype(BF16)
        for s in range(slabs_per_block):
            slab = (src_cb % 2) * slabs_per_block + s
            slab_scr[slab] = res[:, s * LANES:(s + 1) * LANES]
            for r4 in range(4):
                by4 = slab_scr[slab, pl.ds(r4, tm // 4, stride=4), :]
                if d4_cb is not None:
                    d4_ref[r4, :, d4_cb * WIDTH + s * LANES:d4_cb * WIDTH + (s + 1) * LANES] = by4.astype(BF16)
                if d16_cb is not None:
                    slab4_scr[slab, r4 * (tm // 4):(r4 + 1) * (tm // 4), :] = by4
            if d16_cb is not None:
                for r4 in range(4):
                    for r in range(4):
                        by16 = slab4_scr[slab, pl.ds(r4 * (tm // 4) + r, tm // 16, stride=4), :]
                        d16_ref[r4 + 4 * r, :, d16_cb * WIDTH + s * LANES:d16_cb * WIDTH + (s + 1) * LANES] = (
                            by16.astype(BF16))


def _proj_c_call(layer, h, w_in, batch, seq):
    n = h.shape[0]
    tm = PROJ_C_TM
    tiles_per_seq = seq // tm
    n_slabs = 2 * WIDTH // LANES

    def dil_spec(dil):
        return pl.BlockSpec((None, dil, tm // dil, DIL_COLS),
                            lambda i: (i // tiles_per_seq, 0, i % tiles_per_seq, 0))

    def dil_shape(dil):
        return jax.ShapeDtypeStruct((batch, dil, seq // dil, DIL_COLS), BF16)

    return pl.pallas_call(
        _proj_c_kernel,
        grid=(n // tm,),
        in_specs=[pl.BlockSpec((tm, D_MODEL), lambda i: (i, 0)),
                  *[pl.BlockSpec((None, D_MODEL, WIDTH), lambda i, cb=cb: (layer, 0, cb),
                                 pipeline_mode=pl.Buffered(1))
                    for cb in (CB_Q1, CB_Q2, CB_K, CB_V)]],
        out_specs=[pl.BlockSpec((tm, 2 * WIDTH), lambda i: (i, 0)), dil_spec(4), dil_spec(16)],
        out_shape=[jax.ShapeDtypeStruct((n, 2 * WIDTH), BF16), dil_shape(4), dil_shape(16)],
        scratch_shapes=[pltpu.VMEM((D_MODEL, 4 * WIDTH), BF16),
                        pltpu.VMEM((n_slabs, tm, LANES), F32), pltpu.VMEM((n_slabs, tm, LANES), F32)],
        compiler_params=pltpu.CompilerParams(
            dimension_semantics=("arbitrary",), vmem_limit_bytes=VMEM_LIMIT),
        name="proj_qkv",
    )(h, w_in, w_in, w_in, w_in)


def _memkv_kernel(mem_ref, g_ref, w_ref, o_ref):
    mn = _rms_norm(mem_ref[...], g_ref[...]).astype(BF16)
    o_ref[...] = _dot(mn, w_ref[...].astype(BF16)).astype(BF16)


def _memkv_call(mem, g, w):
    depth = w.shape[0]
    b = mem.shape[0]
    return pl.pallas_call(
        _memkv_kernel,
        grid=(depth, b),
        in_specs=[
            pl.BlockSpec((None, MEM_LEN, D_MODEL), lambda l, i: (i, 0, 0)),
            pl.BlockSpec((None, 1, D_MODEL), lambda l, i: (l, 0, 0)),
            pl.BlockSpec((None, D_MODEL, 2 * WIDTH), lambda l, i: (l, 0, 0)),
        ],
        out_specs=pl.BlockSpec((None, None, MEM_LEN, 2 * WIDTH), lambda l, i: (l, i, 0, 0)),
        out_shape=jax.ShapeDtypeStruct((depth, b, MEM_LEN, 2 * WIDTH), BF16),
        compiler_params=pltpu.CompilerParams(
            dimension_semantics=("arbitrary", "arbitrary"), vmem_limit_bytes=VMEM_LIMIT),
        name="memkv",
    )(mem, g.reshape(depth, 1, D_MODEL), w)


def _att_kernel(q_ref, kc_ref, kp_ref, vc_ref, vp_ref, o_ref, lse_ref, s_scr, m_scr, e_scr, *, nres, tq):
    has_prev = pl.program_id(2) > 0
    row = lax.broadcasted_iota(jnp.int32, (CHUNK, 2 * CHUNK), 0)
    col = lax.broadcasted_iota(jnp.int32, (CHUNK, 2 * CHUNK), 1)
    band = jnp.logical_and(col >= row, col <= row + CHUNK)
    first_band = jnp.logical_and(band, jnp.logical_or(has_prev, col >= CHUNK))
    lane_head = lax.broadcasted_iota(jnp.int32, (CHUNK, LANES), 1) // LSE_LANES_PER_HEAD
    ones = jnp.ones((2 * CHUNK, HEAD_DIM), BF16)
    scale = HEAD_DIM ** -0.5
    blocks = [(rr, i) for rr in range(nres) for i in range(tq // CHUNK)]

    def key_rows(ref_cur, ref_prev, rr, i, cols):
        if i == 0:
            return jnp.concatenate([ref_prev[rr, :, cols], ref_cur[rr, 0:CHUNK, cols]], axis=0)
        return ref_cur[rr, (i - 1) * CHUNK:(i + 1) * CHUNK, cols]

    for bi, (rr, i) in enumerate(blocks):
        for h in range(HEADS):
            cols = slice(h * HEAD_DIM, (h + 1) * HEAD_DIM)
            keys = key_rows(kc_ref, kp_ref, rr, i, cols)
            s = _dot_nt(q_ref[rr, i * CHUNK:(i + 1) * CHUNK, cols], keys) * scale
            s = jnp.where(first_band if i == 0 else band, s, NEG)
            s_scr[bi * HEADS + h] = s
            m = jnp.max(jnp.maximum(s[:, :CHUNK], s[:, CHUNK:]), axis=-1, keepdims=True)
            m_scr[bi * HEADS + h] = jnp.broadcast_to(m, (CHUNK, HEAD_DIM))
    for idx in range(len(blocks) * HEADS):
        m = m_scr[idx]
        e_scr[idx] = jnp.exp(s_scr[idx] - jnp.concatenate([m, m], axis=1)).astype(BF16)
    for bi, (rr, i) in enumerate(blocks):
        rows = slice(i * CHUNK, (i + 1) * CHUNK)
        lse = None
        for h in range(HEADS):
            cols = slice(h * HEAD_DIM, (h + 1) * HEAD_DIM)
            vals = jnp.concatenate([key_rows(vc_ref, vp_ref, rr, i, cols), ones], axis=1)
            r = _dot(e_scr[bi * HEADS + h], vals)
            den = r[:, HEAD_DIM:]
            o_ref[rr, rows, cols] = (r[:, :HEAD_DIM] / den).astype(BF16)
            lse_h = m_scr[bi * HEADS + h] + jnp.log(den)
            lse = lse_h if h == 0 else jnp.where(lane_head == h, lse_h, lse)
        lse_ref[rr, rows, :] = lse


def _att_call(dil, q_src, cq, k_src, ck, v_src, cv):
    batch, _, sub_len, _ = q_src.shape
    nres, tq = ATT_BLOCKING[dil]
    tq = min(tq, sub_len)
    blocks_per_step = tq // CHUNK
    n_triples = nres * blocks_per_step * HEADS

    def cur(cb):
        return pl.BlockSpec((None, nres, tq, WIDTH), lambda b, r, n: (b, r, n, cb))

    def prev(cb):
        return pl.BlockSpec((None, nres, CHUNK, WIDTH),
                            lambda b, r, n: (b, r, jnp.maximum(n * blocks_per_step - 1, 0), cb))

    return pl.pallas_call(
        functools.partial(_att_kernel, nres=nres, tq=tq),
        grid=(batch, dil // nres, sub_len // tq),
        in_specs=[cur(cq), cur(ck), prev(ck), cur(cv), prev(cv)],
        out_specs=[pl.BlockSpec((None, nres, tq, WIDTH), lambda b, r, n: (b, r, n, 0)),
                   pl.BlockSpec((None, nres, tq, LANES), lambda b, r, n: (b, r, n, 0))],
        out_shape=[jax.ShapeDtypeStruct((batch, dil, sub_len, WIDTH), BF16),
                   jax.ShapeDtypeStruct((batch, dil, sub_len, LANES), F32)],
        scratch_shapes=[pltpu.VMEM((n_triples, CHUNK, 2 * CHUNK), F32),
                        pltpu.VMEM((n_triples, CHUNK, HEAD_DIM), F32),
                        pltpu.VMEM((n_triples, CHUNK, 2 * CHUNK), BF16)],
        compiler_params=pltpu.CompilerParams(
            dimension_semantics=("arbitrary", "arbitrary", "arbitrary"), vmem_limit_bytes=VMEM_LIMIT),
        name=f"att_d{dil}",
    )(q_src, k_src, k_src, v_src, v_src)


def _post_kernel(u_ref, vn_ref, sa_ref, pi_ref, ph_ref, sp_ref, sc_ref, mq_ref, sm_ref, gate_ref,
                 o1_ref, l1_ref, o2_ref, l2_ref, o3_ref, l3_ref, kvm_ref, x_ref,
                 ws_ref, bs_ref, pw_ref, ps_ref, wb_ref, wo_ref, fg_ref,
                 out_ref, y_scr, nat_scr, lse_scr, z_scr, *, t, tiles_per_seq, final_norm):
    tile_in_seq = pl.program_id(0) % tiles_per_seq
    sub = POST_SUB
    pieces = [(si * sub, slice(si * sub, (si + 1) * sub)) for si in range(t // sub)]

    def branch_a(r0, rs):
        row = lax.broadcasted_iota(jnp.int32, (CHUNK, CHUNK), 0)
        col = lax.broadcasted_iota(jnp.int32, (CHUNK, CHUNK), 1)
        for h in range(HEADS):
            cols = slice(h * HEAD_DIM, (h + 1) * HEAD_DIM)
            w_causal = jnp.where(col <= row, ws_ref[h], 0.0).astype(BF16)
            for c in range(sub // CHUNK):
                rows = slice(r0 + c * CHUNK, r0 + (c + 1) * CHUNK)
                mixed = _dot(w_causal, vn_ref[rows, cols]) + bs_ref[h]
                y_scr[0, rows, cols] = (
                    u_ref[rows, cols].astype(F32) * mixed * sa_ref[rows, cols].astype(F32)).astype(BF16)

    def branch_b(r0, rs):
        p = pi_ref[rs, :].astype(F32)
        if r0 == 0:
            halo = jnp.where(tile_in_seq == 0, 0.0, ph_ref[...].astype(F32))
        else:
            halo = pi_ref[r0 - POOL_HALO:r0, :].astype(F32)
        ext = jnp.concatenate([halo, p], axis=0)
        pos = tile_in_seq * t + r0 + lax.broadcasted_iota(jnp.int32, (sub, 1), 0)
        for g, win in enumerate(POOL_WINDOWS):
            cols = slice(g * HEAD_DIM, (g + 1) * HEAD_DIM)
            acc = ext[:, cols]
            k = 1
            while k < win:
                acc = acc + pltpu.roll(acc, k, axis=0)
                k *= 2
            count = jnp.minimum(pos + 1, win).astype(F32)
            d = acc[POOL_HALO:, :] / count - p[:, cols]
            y = _dot(d.astype(BF16), pw_ref[g]) * ps_ref[:, cols]
            y_scr[1, rs, cols] = (y * sp_ref[rs, cols].astype(F32)).astype(BF16)

    def branch_c(r0, rs):
        for a, (o_ref, l_ref, dil) in enumerate(((o2_ref, l2_ref, 4), (o3_ref, l3_ref, 16))):
            drows = slice(r0 // dil, (r0 + sub) // dil)
            for r in range(dil):
                lse_scr[a, pl.ds(r0 + r, sub // dil, stride=dil), :] = l_ref[r, drows, :]
                for h in range(HEADS):
                    nat_scr[a * HEADS + h, pl.ds(r0 + r, sub // dil, stride=dil), :] = (
                        o_ref[r, drows, h * HEAD_DIM:(h + 1) * HEAD_DIM].astype(F32))
        l1, l2, l3 = l1_ref[rs, :], lse_scr[0, rs, :], lse_scr[1, rs, :]
        lmax = jnp.maximum(jnp.maximum(l1, l2), l3)
        e1, e2, e3 = jnp.exp(l1 - lmax), jnp.exp(l2 - lmax), jnp.exp(l3 - lmax)
        inv = 1.0 / (e1 + e2 + e3)
        w1, w2, w3 = e1 * inv, e2 * inv, e3 * inv
        for h in range(HEADS):
            cols = slice(h * HEAD_DIM, (h + 1) * HEAD_DIM)
            lane = slice(h * LSE_LANES_PER_HEAD, h * LSE_LANES_PER_HEAD + 1)
            b1, b2, b3 = (jnp.broadcast_to(w[:, lane], (sub, HEAD_DIM)) for w in (w1, w2, w3))
            oc = (b1 * o1_ref[rs, cols].astype(F32) + b2 * nat_scr[h, rs, :] + b3 * nat_scr[HEADS + h, rs, :])
            y_scr[2, rs, cols] = (oc * sc_ref[rs, cols].astype(F32)).astype(BF16)

    def branch_m(r0, rs):
        mem_ones = jnp.ones((MEM_LEN, HEAD_DIM), BF16)
        scale = HEAD_DIM ** -0.5
        for h in range(HEADS):
            cols = slice(h * HEAD_DIM, (h + 1) * HEAD_DIM)
            s = _dot_nt(mq_ref[rs, cols], kvm_ref[:, cols]) * scale
            m = jnp.max(jnp.maximum(s[:, :MEM_LEN // 2], s[:, MEM_LEN // 2:]), axis=-1, keepdims=True)
            e = jnp.exp(s - m).astype(BF16)
            vals = jnp.concatenate(
                [kvm_ref[:, WIDTH + h * HEAD_DIM:WIDTH + (h + 1) * HEAD_DIM], mem_ones], axis=1)
            r = _dot(e, vals)
            om = r[:, :HEAD_DIM] / r[:, HEAD_DIM:]
            y_scr[3, rs, cols] = (om * sm_ref[rs, cols].astype(F32)).astype(BF16)

    for b, branch in enumerate((branch_a, branch_b, branch_c, branch_m)):
        for r0, rs in pieces:
            branch(r0, rs)
        gated = gate_ref[:, b * D_MODEL:(b + 1) * D_MODEL].astype(F32) * _dot(y_scr[b], wb_ref[b])
        if b == 0:
            z_scr[...] = gated
        else:
            z_scr[...] += gated
    x_new = x_ref[...] + _dot(z_scr[...].astype(BF16), wo_ref[...])
    out_ref[...] = _rms_norm(x_new, fg_ref[...]) if final_norm else x_new


def _post_call(layer, pa, pb, pd, gates, atts, kvm, x, batch, seq, ws, bs, pw, ps, wb, wo, fg, final_norm):
    n = x.shape[0]
    t = POST_T
    tiles_per_seq = seq // t
    halo_blocks_per_tile = t // POOL_HALO

    def col_blk(cb):
        return pl.BlockSpec((t, WIDTH), lambda i: (i, cb))

    def layer_blk(*shape):
        return pl.BlockSpec((None,) + shape, lambda i: (layer,) + (0,) * len(shape))

    def dil_blk(dil, width):
        return pl.BlockSpec((None, dil, t // dil, width),
                            lambda i: (i // tiles_per_seq, 0, i % tiles_per_seq, 0))

    halo_spec = pl.BlockSpec((POOL_HALO, WIDTH), lambda i: (jnp.maximum(i * halo_blocks_per_tile - 1, 0), 0))
    (o1, l1), (o2, l2), (o3, l3) = atts
    operands = [
        (pa, col_blk(0)), (pa, col_blk(1)), (pa, col_blk(2)),
        (pb, col_blk(0)), (pb, halo_spec), (pb, col_blk(1)),
        (pd, col_blk(0)), (pd, col_blk(1)), (pd, col_blk(2)),
        (gates, pl.BlockSpec((t, N_BRANCH * D_MODEL), lambda i: (i, 0))),
        (o1.reshape(n, WIDTH), col_blk(0)), (l1.reshape(n, LANES), pl.BlockSpec((t, LANES), lambda i: (i, 0))),
        (o2, dil_blk(4, WIDTH)), (l2, dil_blk(4, LANES)), (o3, dil_blk(16, WIDTH)), (l3, dil_blk(16, LANES)),
        (kvm, pl.BlockSpec((None, None, MEM_LEN, 2 * WIDTH), lambda i: (layer, i // tiles_per_seq, 0, 0))),
        (x, pl.BlockSpec((t, D_MODEL), lambda i: (i, 0))),
        (ws, layer_blk(HEADS, CHUNK, CHUNK)), (bs, layer_blk(HEADS, CHUNK, CHUNK)),
        (pw, layer_blk(len(POOL_WINDOWS), HEAD_DIM, HEAD_DIM)), (ps, layer_blk(1, WIDTH)),
        (wb, layer_blk(N_BRANCH, WIDTH, D_MODEL)), (wo, layer_blk(D_MODEL, D_MODEL)),
        (fg, pl.BlockSpec((1, D_MODEL), lambda i: (0, 0))),
    ]
    return pl.pallas_call(
        functools.partial(_post_kernel, t=t, tiles_per_seq=tiles_per_seq, final_norm=final_norm),
        grid=(n // t,),
        in_specs=[spec for _, spec in operands],
        out_specs=pl.BlockSpec((t, D_MODEL), lambda i: (i, 0)),
        out_shape=jax.ShapeDtypeStruct((n, D_MODEL), F32),
        scratch_shapes=[pltpu.VMEM((N_BRANCH, t, WIDTH), BF16), pltpu.VMEM((2 * HEADS, t, LANES), F32),
                        pltpu.VMEM((2, t, LANES), F32), pltpu.VMEM((t, D_MODEL), F32)],
        compiler_params=pltpu.CompilerParams(
            dimension_semantics=("arbitrary",), vmem_limit_bytes=VMEM_LIMIT),
        name="post",
    )(*[arr for arr, _ in operands])


def kernel(x, mem, norm_g, w_in, gm_ln_g, gm_ln_b, gm_ws, gm_bs, pool_w, pool_scale, mem_norm_g,
           w_mem_kv, w_branch, w_out, final_norm_g):
    batch, seq, d_model = x.shape
    depth = w_in.shape[0]
    assert d_model == D_MODEL and w_in.shape[2] == D_IN and mem.shape[1] == MEM_LEN
    assert all(window // dil == CHUNK for window, dil in DIL_PATTERNS)
    assert seq % (DIL_PATTERNS[-1][1] * CHUNK) == 0 and seq % PROJ_TM == 0 and seq % POST_T == 0
    n = batch * seq

    wb = w_branch.astype(BF16)
    wo = w_out.astype(BF16)
    pw = pool_w.astype(BF16)
    bs_tile = jnp.broadcast_to(gm_bs[:, :, :, None], gm_bs.shape + (CHUNK,))
    lng = gm_ln_g.reshape(depth, 1, WIDTH)
    lnb = gm_ln_b.reshape(depth, 1, WIDTH)
    ps = pool_scale.reshape(depth, 1, WIDTH)
    ng = norm_g.reshape(depth, 1, D_MODEL)
    fg = final_norm_g.reshape(1, D_MODEL)

    kvm = _memkv_call(mem, mem_norm_g, w_mem_kv)
    xf = x.reshape(n, D_MODEL)
    for layer in range(depth):
        h, pa = _proj_call(layer, GROUP_A, xf, ng, w_in, lng, lnb, from_x=True)
        (pb,) = _proj_call(layer, GROUP_B, h, ng, w_in, lng, lnb)
        (pd,) = _proj_call(layer, GROUP_D, h, ng, w_in, lng, lnb)
        (gates,) = _proj_call(layer, GROUP_E, h, ng, w_in, lng, lnb, tn=PROJ_E_TN)
        kv, dil4, dil16 = _proj_c_call(layer, h, w_in, batch, seq)
        atts = [_att_call(1, pb.reshape(batch, 1, seq, 3 * WIDTH), 2,
                          kv.reshape(batch, 1, seq, 2 * WIDTH), KV_K, kv.reshape(batch, 1, seq, 2 * WIDTH), KV_V),
                _att_call(4, dil4, DIL_Q, dil4, DIL_K, dil4, DIL_V),
                _att_call(16, dil16, DIL_Q, dil16, DIL_K, dil16, DIL_V)]
        xf = _post_call(layer, pa, pb, pd, gates, atts, kvm, xf, batch, seq,
                        gm_ws, bs_tile, pw, ps, wb, wo, fg, final_norm=(layer == depth - 1))
    return xf.reshape(batch, seq, D_MODEL)
```
